```python
import math
import jax
import jax.numpy as jnp
from jax import lax
import numpy as np

D_MODEL = 2048
BATCH = 4
SEQ = 4096
DEPTH = 2

CTX_LEN = 256
GRID_W = 64
N_MOD = 9
D_FF = 5632
MACARON_WEIGHT = 0.5
GLA_HEADS = 4
GLA_DK = 64
GLA_DV = 128
GLA_RANK = 16
GLA_TAU = 16.0
GLA_CHUNK = 64
DN_HEADS = 4
DN_DK = 128
DN_DV = 128
DN_CONV = 3
DN_CHUNK = 64
ATT_HEADS = 8
ATT_KV_HEADS = 2
ATT_HD = 128
Q_BLOCK = 128
ROPE_THETA = 10000.0
GLA_QK_W = GLA_HEADS * GLA_DK
GLA_V_W = GLA_HEADS * GLA_DV
DN_QK_W = DN_HEADS * DN_DK
DN_V_W = DN_HEADS * DN_DV
ATT_Q_W = ATT_HEADS * ATT_HD
ATT_KV_W = ATT_KV_HEADS * ATT_HD
MIX_WIDTH = GLA_V_W + DN_V_W + ATT_Q_W
IN_WIDTHS = (GLA_QK_W, GLA_QK_W, GLA_V_W, GLA_V_W, DN_QK_W, DN_QK_W, DN_V_W, DN_V_W, ATT_Q_W, ATT_KV_W, ATT_KV_W)
IN_WIDTH = sum(IN_WIDTHS)
DEEPNORM_ALPHA = (2 * DEPTH) ** 0.25
DEEPNORM_BETA = (8 * DEPTH) ** -0.25
NORM_EPS = 1e-6

kernel_name = 'hymba_gla_gdn_gqa_macaron_deepnorm_dit'


def _layernorm(x, g, b):
    xf = x.astype(jnp.float32)
    mu = jnp.mean(xf, axis=-1, keepdims=True)
    xc = xf - mu
    var = jnp.mean(xc * xc, axis=-1, keepdims=True)
    return (xc * lax.rsqrt(var + NORM_EPS) * g + b).astype(x.dtype)


def _rmsnorm(x, g):
    xf = x.astype(jnp.float32)
    return (xf * lax.rsqrt(jnp.mean(xf * xf, axis=-1, keepdims=True) + NORM_EPS) * g).astype(x.dtype)


def _l2norm(x):
    xf = x.astype(jnp.float32)
    return (xf * lax.rsqrt(jnp.sum(xf * xf, axis=-1, keepdims=True) + NORM_EPS)).astype(x.dtype)


def _modulate(x, shift, scale):
    return x * (1.0 + scale) + shift


def _post_norm(x, y, gate, g, b):
    return _layernorm(DEEPNORM_ALPHA * x + gate * y, g, b)


def _swiglu(h, wg, wu, wd):
    return (jax.nn.silu(h @ wg) * (h @ wu)) @ wd


def _ffn_sublayer(x, shift, scale, gate, wg, wu, wd, g, b):
    h = _modulate(x, shift, scale)
    return _post_norm(x, MACARON_WEIGHT * _swiglu(h, wg, wu, wd), gate, g, b)


def _seg_reverse(z, n_ctx):
    return jnp.concatenate([jnp.flip(z[:, :n_ctx], axis=1), jnp.flip(z[:, n_ctx:], axis=1)], axis=1)


def _directional(fn, arrays, n_ctx, backward):
    if backward:
        out = fn(*[_seg_reverse(a, n_ctx) for a in arrays])
        return _seg_reverse(out, n_ctx)
    return fn(*arrays)


def _to_chunks(z, c):
    b, t, h, d = z.shape
    return z.reshape(b, t // c, c, h, d).transpose(1, 0, 3, 2, 4)


def _from_chunks(z):
    n, b, h, c, d = z.shape
    return z.transpose(1, 0, 3, 2, 4).reshape(b, n * c, h, d)


def _gla_chunked(q, k, v, log_a):
    bsz, _, nh, dk = q.shape
    dv = v.shape[-1]
    f32 = jnp.float32
    qc, kc, vc, gc = (_to_chunks(t.astype(f32), GLA_CHUNK) for t in (q, k, v, log_a))
    bcum = jnp.cumsum(gc, axis=-2)
    blast = bcum[..., -1:, :]
    q_in = qc * jnp.exp(bcum)
    k_in = kc * jnp.exp(-bcum)
    k_end = kc * jnp.exp(blast - bcum)
    idx = jnp.arange(GLA_CHUNK)
    lower = idx[:, None] >= idx[None, :]
    attn = jnp.where(lower, jnp.einsum('nbhid,nbhjd->nbhij', q_in, k_in), 0.0)
    o_intra = jnp.einsum('nbhij,nbhjv->nbhiv', attn, vc)

    def step(state, inp):
        qi, ke, vi, bl = inp
        o = jnp.einsum('bhid,bhdv->bhiv', qi, state)
        state = state * jnp.exp(bl)[..., 0, :, None] + jnp.einsum('bhjd,bhjv->bhdv', ke, vi)
        return state, o

    s0 = jnp.zeros((bsz, nh, dk, dv), f32)
    _, o_inter = lax.scan(step, s0, (q_in, k_end, vc, blast))
    return _from_chunks(o_intra + o_inter).astype(v.dtype)


def _unit_lower_inverse(a):
    c = a.shape[-1]
    n = -a
    t = jnp.eye(c, dtype=a.dtype) + n
    p = n
    for _ in range(int(math.log2(c)) - 1):
        p = p @ p
        t = t + t @ p
    return t


def _gated_delta_chunked(q, k, v, beta, g):
    bsz, _, nh, dk = q.shape
    dv = v.shape[-1]
    f32 = jnp.float32
    qc, kc, vc = (_to_chunks(t.astype(f32), DN_CHUNK) for t in (q, k, v))
    bc = _to_chunks(beta[..., None].astype(f32), DN_CHUNK)
    gcum = jnp.cumsum(_to_chunks(g[..., None].astype(f32), DN_CHUNK), axis=-2)
    glast = gcum[..., -1:, :]
    diff = gcum - jnp.swapaxes(gcum, -1, -2)
    idx = jnp.arange(DN_CHUNK)
    dec_strict = jnp.exp(jnp.where(idx[:, None] > idx[None, :], diff, -jnp.inf))
    dec_incl = jnp.exp(jnp.where(idx[:, None] >= idx[None, :], diff, -jnp.inf))
    kb = kc * bc
    tinv = _unit_lower_inverse(jnp.einsum('nbhid,nbhjd->nbhij', kb, kc) * dec_strict)
    u = tinv @ (vc * bc)
    w = tinv @ (kb * jnp.exp(gcum))
    attn = jnp.einsum('nbhid,nbhjd->nbhij', qc, kc) * dec_incl
    q_dec = qc * jnp.exp(gcum)
    k_end = kc * jnp.exp(glast - gcum)

    def step(state, inp):
        ui, wi, ai, qi, ki, gl = inp
        v_new = ui - wi @ state
        o = qi @ state + ai @ v_new
        state = state * jnp.exp(gl) + jnp.einsum('bhjd,bhjv->bhdv', ki, v_new)
        return state, o

    s0 = jnp.zeros((bsz, nh, dk, dv), f32)
    _, o = lax.scan(step, s0, (u, w, attn, q_dec, k_end, glast))
    return _from_chunks(o).astype(v.dtype)


def _centred_dwconv(z, w):
    k = w.shape[0]
    return lax.conv_general_dilated(z, w[:, None, :].astype(z.dtype), window_strides=(1,),
                                    padding=[(k // 2, k // 2)],
                                    dimension_numbers=('NWC', 'WIO', 'NWC'),
                                    feature_group_count=z.shape[-1])


def _axial_rope_tables(n_lat):
    rows = n_lat // GRID_W
    pr = jnp.broadcast_to(jnp.arange(rows, dtype=jnp.float32)[:, None], (rows, GRID_W)).reshape(-1)
    pc = jnp.broadcast_to(jnp.arange(GRID_W, dtype=jnp.float32)[None, :], (rows, GRID_W)).reshape(-1)
    axis_dim = ATT_HD // 2
    inv = ROPE_THETA ** (-jnp.arange(0, axis_dim, 2, dtype=jnp.float32) / axis_dim)
    ar = pr[:, None] * inv
    ac = pc[:, None] * inv
    return (jnp.cos(ar), jnp.sin(ar), jnp.cos(ac), jnp.sin(ac))


def _apply_axial_rope(x, tabs):
    cr, sr, cc, sc = (t[:, None, :].astype(x.dtype) for t in tabs)
    r1, r2, c1, c2 = jnp.split(x, 4, axis=-1)
    return jnp.concatenate([r1 * cr - r2 * sr, r2 * cr + r1 * sr,
                            c1 * cc - c2 * sc, c2 * cc + c1 * sc], axis=-1)


def _gqa_attend(q, k, v):
    bsz, nq, nh, hd = q.shape
    kvh = k.shape[2]
    qg = q.reshape(bsz, nq, kvh, nh // kvh, hd)
    s = jnp.einsum('bqkgd,bskd->bkgqs', qg, k, preferred_element_type=jnp.float32) * (hd ** -0.5)
    p = jax.nn.softmax(s, axis=-1).astype(v.dtype)
    o = jnp.einsum('bkgqs,bskd->bqkgd', p, v)
    return o.reshape(bsz, nq, nh * hd)


def _gqa_blocked(q, k, v):
    bsz, nq, nh, hd = q.shape
    qb = q.reshape(bsz, nq // Q_BLOCK, Q_BLOCK, nh, hd).transpose(1, 0, 2, 3, 4)
    ob = lax.map(lambda qi: _gqa_attend(qi, k, v), qb)
    return ob.transpose(1, 0, 2, 3).reshape(bsz, nq, nh * hd)


def _mixer(hc, hl, w_in, gla_wa1, gla_wa2, gla_ba, gla_norm_g, dn_conv, dn_wab, dn_a_log,
           dn_dt_bias, dn_norm_g, q_norm_g, k_norm_g, w_out, need_ctx):
    bsz, n_ctx, _ = hc.shape
    n_lat = hl.shape[1]
    t = n_ctx + n_lat
    h = jnp.concatenate([hc, hl], axis=1)
    offs = np.cumsum(IN_WIDTHS)[:-1].tolist()
    (g_q, g_k, g_v, g_r, d_q, d_k, d_v, d_gate, a_q, a_k, a_v) = jnp.split(h @ w_in, offs, axis=-1)

    gq = g_q.reshape(bsz, t, GLA_HEADS, GLA_DK) * (GLA_DK ** -0.5)
    gk = g_k.reshape(bsz, t, GLA_HEADS, GLA_DK)
    gv = g_v.reshape(bsz, t, GLA_HEADS, GLA_DV)
    gla_dirs = []
    for d in range(2):
        logit = ((h @ gla_wa1[d]) @ gla_wa2[d] + gla_ba[d]).astype(jnp.float32)
        log_a = (jax.nn.log_sigmoid(logit) / GLA_TAU).reshape(bsz, t, GLA_HEADS, GLA_DK)
        gla_dirs.append(_directional(_gla_chunked, (gq, gk, gv, log_a), n_ctx, d == 1))
    gla_o = _rmsnorm(gla_dirs[0] + gla_dirs[1], gla_norm_g) * jax.nn.silu(g_r.reshape(bsz, t, GLA_HEADS, GLA_DV))
    gla_o = gla_o.reshape(bsz, t, GLA_V_W)

    qkv = jnp.concatenate([d_q, d_k, d_v], axis=-1)
    qkv = jax.nn.silu(jnp.concatenate([_centred_dwconv(qkv[:, :n_ctx], dn_conv),
                                       _centred_dwconv(qkv[:, n_ctx:], dn_conv)], axis=1))
    dq, dk, dv = jnp.split(qkv, [DN_QK_W, 2 * DN_QK_W], axis=-1)
    dq = _l2norm(dq.reshape(bsz, t, DN_HEADS, DN_DK)) * (DN_DK ** -0.5)
    dk = _l2norm(dk.reshape(bsz, t, DN_HEADS, DN_DK))
    dv = dv.reshape(bsz, t, DN_HEADS, DN_DV)
    dn_dirs = []
    for d in range(2):
        a_in, b_in = jnp.split((h @ dn_wab[d]).astype(jnp.float32), 2, axis=-1)
        g = -jnp.exp(dn_a_log[d].astype(jnp.float32)) * jax.nn.softplus(a_in + dn_dt_bias[d].astype(jnp.float32))
        beta = jax.nn.sigmoid(b_in)
        dn_dirs.append(_directional(_gated_delta_chunked, (dq, dk, dv, beta, g), n_ctx, d == 1))
    dn_o = _rmsnorm(dn_dirs[0] + dn_dirs[1], dn_norm_g) * jax.nn.silu(d_gate.reshape(bsz, t, DN_HEADS, DN_DV))
    dn_o = dn_o.reshape(bsz, t, DN_V_W)

    aq = _rmsnorm(a_q.reshape(bsz, t, ATT_HEADS, ATT_HD), q_norm_g)
    ak = _rmsnorm(a_k.reshape(bsz, t, ATT_KV_HEADS, ATT_HD), k_norm_g)
    av = a_v.reshape(bsz, t, ATT_KV_HEADS, ATT_HD)
    rope = _axial_rope_tables(n_lat)
    q_lat = _apply_axial_rope(aq[:, n_ctx:], rope)
    k_all = jnp.concatenate([ak[:, :n_ctx], _apply_axial_rope(ak[:, n_ctx:], rope)], axis=1)
    att_lat = _gqa_blocked(q_lat, k_all, av)
    y_lat = jnp.concatenate([gla_o[:, n_ctx:], dn_o[:, n_ctx:], att_lat], axis=-1) @ w_out
    if not need_ctx:
        return None, y_lat
    att_ctx = _gqa_attend(aq[:, :n_ctx], ak[:, :n_ctx], av[:, :n_ctx])
    y_ctx = jnp.concatenate([gla_o[:, :n_ctx], dn_o[:, :n_ctx], att_ctx], axis=-1) @ w_out
    return y_ctx, y_lat


def setup_inputs(seed: int = 0) -> dict:
    key = jax.random.key(seed)
    ks = jax.random.split(key, 24)
    f32 = jnp.float32

    def nrm(k, shape, scale):
        return jax.random.normal(k, shape, f32) * scale

    nl = DEPTH
    dt = jnp.exp(jax.random.uniform(ks[19], (nl, 2, DN_HEADS), f32, math.log(1e-3), math.log(1e-1)))
    return {
        'x': nrm(ks[0], (BATCH, SEQ, D_MODEL), 1.0),
        'c': nrm(ks[1], (BATCH, D_MODEL), 1.0),
        'ctx': nrm(ks[2], (BATCH, CTX_LEN, D_MODEL), 1.0),
        'c_ctx': nrm(ks[3], (D_MODEL,), 1.0),
        'w_ada': nrm(ks[4], (nl, D_MODEL, N_MOD * D_MODEL), 0.5 * D_MODEL ** -0.5),
        'b_ada': nrm(ks[5], (nl, N_MOD * D_MODEL), 0.02),
        'ln_g': 1.0 + nrm(ks[6], (nl, 3, D_MODEL), 0.02),
        'ln_b': nrm(ks[7], (nl, 3, D_MODEL), 0.02),
        'w_ffn_gate': nrm(ks[8], (nl, 2, D_MODEL, D_FF), D_MODEL ** -0.5),
        'w_ffn_up': nrm(ks[9], (nl, 2, D_MODEL, D_FF), D_MODEL ** -0.5),
        'w_ffn_down': nrm(ks[10], (nl, 2, D_FF, D_MODEL), DEEPNORM_BETA * D_FF ** -0.5),
        'w_in': nrm(ks[11], (nl, D_MODEL, IN_WIDTH), D_MODEL ** -0.5),
        'gla_wa1': nrm(ks[12], (nl, 2, D_MODEL, GLA_RANK), D_MODEL ** -0.5),
        'gla_wa2': nrm(ks[13], (nl, 2, GLA_RANK, GLA_QK_W), GLA_RANK ** -0.5),
        'gla_ba': nrm(ks[14], (nl, 2, GLA_QK_W), 0.1),
        'gla_norm_g': 1.0 + nrm(ks[15], (nl, GLA_DV), 0.02),
        'dn_conv': nrm(ks[16], (nl, DN_CONV, 2 * DN_QK_W + DN_V_W), DN_CONV ** -0.5),
        'dn_wab': nrm(ks[17], (nl, 2, D_MODEL, 2 * DN_HEADS), D_MODEL ** -0.5),
        'dn_a_log': jnp.log(jax.random.uniform(ks[18], (nl, 2, DN_HEADS), f32, 1.0, 16.0)),
        'dn_dt_bias': dt + jnp.log(-jnp.expm1(-dt)),
        'dn_norm_g': 1.0 + nrm(ks[20], (nl, DN_DV), 0.02),
        'q_norm_g': 1.0 + nrm(ks[21], (nl, ATT_HD), 0.02),
        'k_norm_g': 1.0 + nrm(ks[22], (nl, ATT_HD), 0.02),
        'w_out': nrm(ks[23], (nl, MIX_WIDTH, D_MODEL), DEEPNORM_BETA * MIX_WIDTH ** -0.5),
    }


def reference(x, c, ctx, c_ctx, w_ada, b_ada, ln_g, ln_b, w_ffn_gate, w_ffn_up, w_ffn_down, w_in,
              gla_wa1, gla_wa2, gla_ba, gla_norm_g, dn_conv, dn_wab, dn_a_log, dn_dt_bias, dn_norm_g,
              q_norm_g, k_norm_g, w_out):
    bsz = x.shape[0]
    s_c = jax.nn.silu(c)
    s_cc = jax.nn.silu(c_ctx)
    xl, xc = x, ctx
    for layer in range(DEPTH):
        last = layer == DEPTH - 1
        mod_l = (s_c @ w_ada[layer] + b_ada[layer]).reshape(bsz, N_MOD, 1, D_MODEL)
        mod_c = (s_cc @ w_ada[layer] + b_ada[layer]).reshape(N_MOD, D_MODEL)
        xc = _ffn_sublayer(xc, mod_c[0], mod_c[1], mod_c[2], w_ffn_gate[layer, 0], w_ffn_up[layer, 0],
                           w_ffn_down[layer, 0], ln_g[layer, 0], ln_b[layer, 0])
        xl = _ffn_sublayer(xl, mod_l[:, 0], mod_l[:, 1], mod_l[:, 2], w_ffn_gate[layer, 0], w_ffn_up[layer, 0],
                           w_ffn_down[layer, 0], ln_g[layer, 0], ln_b[layer, 0])
        yc, yl = _mixer(_modulate(xc, mod_c[3], mod_c[4]), _modulate(xl, mod_l[:, 3], mod_l[:, 4]),
                        w_in[layer], gla_wa1[layer], gla_wa2[layer], gla_ba[layer], gla_norm_g[layer],
                        dn_conv[layer], dn_wab[layer], dn_a_log[layer], dn_dt_bias[layer], dn_norm_g[layer],
                        q_norm_g[layer], k_norm_g[layer], w_out[layer], not last)
        xl = _post_norm(xl, yl, mod_l[:, 5], ln_g[layer, 1], ln_b[layer, 1])
        xl = _ffn_sublayer(xl, mod_l[:, 6], mod_l[:, 7], mod_l[:, 8], w_ffn_gate[layer, 1], w_ffn_up[layer, 1],
                           w_ffn_down[layer, 1], ln_g[layer, 2], ln_b[layer, 2])
        if not last:
            xc = _post_norm(xc, yc, mod_c[5], ln_g[layer, 1], ln_b[layer, 1])
            xc = _ffn_sublayer(xc, mod_c[6], mod_c[7], mod_c[8], w_ffn_gate[layer, 1], w_ffn_up[layer, 1],
                               w_ffn_down[layer, 1], ln_g[layer, 2], ln_b[layer, 2])
    return xl
```

```python
import functools
import math

import jax
import jax.numpy as jnp
from jax import lax
from jax.experimental import pallas as pl
from jax.experimental.pallas import tpu as pltpu

F32 = jnp.float32
BF16 = jnp.bfloat16

N_MOD = 9
GRID_W = 64
MACARON_WEIGHT = 0.5
GLA_HEADS, GLA_DK, GLA_DV, GLA_RANK, GLA_TAU = 4, 64, 128, 16, 16.0
DN_HEADS, DN_DK, DN_DV, DN_CONV = 4, 128, 128, 3
ATT_HEADS, ATT_KV_HEADS, ATT_HD = 8, 2, 128
ATT_GROUP = ATT_HEADS // ATT_KV_HEADS
CHUNK = 64
ROPE_THETA = 10000.0
NORM_EPS = 1e-6
GLA_QK_W, GLA_V_W = GLA_HEADS * GLA_DK, GLA_HEADS * GLA_DV
DN_QK_W, DN_V_W = DN_HEADS * DN_DK, DN_HEADS * DN_DV
ATT_Q_W, ATT_KV_W = ATT_HEADS * ATT_HD, ATT_KV_HEADS * ATT_HD
IN_WIDTH = 2 * GLA_QK_W + 2 * GLA_V_W + 2 * DN_QK_W + 2 * DN_V_W + ATT_Q_W + 2 * ATT_KV_W
MIX_WIDTH = GLA_V_W + DN_V_W + ATT_Q_W
OFF_GLA_V, OFF_GLA_R = 2 * GLA_QK_W, 2 * GLA_QK_W + GLA_V_W
OFF_DN = OFF_GLA_R + GLA_V_W
OFF_DN_GATE = OFF_DN + 2 * DN_QK_W + DN_V_W
OFF_ATT = OFF_DN_GATE + DN_V_W
EX_WIDTH = 128
EX_DN = 2 * GLA_RANK

V7X_VMEM_BYTES = 64 * 1024 * 1024
LANES = 128
SEQ_BLOCK = 256
CHUNKS_PER_BLOCK = SEQ_BLOCK // CHUNK


def _cparams(n_axes, vmem_bytes):
    limit = int(min(max(vmem_bytes, 16 * 1024 * 1024), V7X_VMEM_BYTES - 8 * 1024 * 1024))
    return pltpu.CompilerParams(dimension_semantics=("arbitrary",) * n_axes,
                                vmem_limit_bytes=limit)


def _dot(a, b):
    return jnp.dot(a, b, preferred_element_type=F32)


def _dot_nt(a, b):
    return lax.dot_general(a, b, (((1,), (1,)), ((), ())), preferred_element_type=F32)


def _dot_tn(a, b):
    return lax.dot_general(a, b, (((0,), (0,)), ((), ())), preferred_element_type=F32)


def _sigmoid(x):
    return 1.0 / (1.0 + jnp.exp(-x))


def _silu(x):
    return x * _sigmoid(x)


def _softplus(x):
    return jnp.maximum(x, 0.0) + jnp.log(1.0 + jnp.exp(-jnp.abs(x)))


def _layernorm(z, g, b):
    mu = jnp.mean(z, axis=-1, keepdims=True)
    zc = z - mu
    var = jnp.mean(zc * zc, axis=-1, keepdims=True)
    return zc * lax.rsqrt(var + NORM_EPS) * g + b


def _head_rmsnorm(x, g):
    return x * lax.rsqrt(jnp.mean(x * x, axis=-1, keepdims=True) + NORM_EPS) * g


def _cumsum_f32(tri, g):
    g1 = g.astype(BF16)
    r1 = g - g1.astype(F32)
    g2 = r1.astype(BF16)
    g3 = (r1 - g2.astype(F32)).astype(BF16)
    return _dot(tri, g1) + _dot(tri, g2) + _dot(tri, g3)


def _pick(n, candidates):
    for c in candidates:
        if n % c == 0:
            return c
    raise ValueError(f"no block size in {candidates} divides {n}")


def _mod_block_index(i, tm, n_lat_rows, seq, n_batch):
    assert seq % tm == 0 and n_lat_rows % tm == 0
    r = i * tm
    return jnp.where(r < n_lat_rows, r // seq, n_batch)


def _ada_kernel(c_ref, w_ref, b_ref, o_ref):
    s = _silu(c_ref[...]).astype(BF16)
    o_ref[0] = _dot(s, w_ref[0].astype(BF16)) + b_ref[0]


def _ada_table(cond, w_ada, b_ada):
    n_layer, d, n = w_ada.shape
    tn = _pick(n, (1024, 512, 256, 128))
    vmem = 2 * d * tn * 4 + d * tn * 2 + 4 * 8 * (d + 2 * tn) * 4
    return pl.pallas_call(
        _ada_kernel,
        out_shape=jax.ShapeDtypeStruct((n_layer, 8, n), F32),
        grid=(n_layer, n // tn),
        in_specs=[pl.BlockSpec((8, d), lambda l, j: (0, 0)),
                  pl.BlockSpec((1, d, tn), lambda l, j: (l, 0, j)),
                  pl.BlockSpec((1, 1, tn), lambda l, j: (l, 0, j))],
        out_specs=pl.BlockSpec((1, 8, tn), lambda l, j: (l, 0, j)),
        compiler_params=_cparams(2, vmem),
        name="ada_table",
    )(cond, w_ada, b_ada.reshape(n_layer, 1, n))


def _ffn_kernel(x_ref, mod_ref, wg_ref, wu_ref, wd_ref, g_ref, b_ref, o_ref, h_ref, acc_ref,
                *, mod_row, alpha):
    j = pl.program_id(1)

    @pl.when(j == 0)
    def _():
        shift = mod_ref[0, mod_row:mod_row + 1, :]
        scale = mod_ref[0, mod_row + 1:mod_row + 2, :]
        h_ref[...] = (x_ref[...] * (1.0 + scale) + shift).astype(BF16)
        acc_ref[...] = jnp.zeros(acc_ref.shape, F32)

    h = h_ref[...]
    a = _dot(h, wg_ref[...])
    u = _dot(h, wu_ref[...])
    acc_ref[...] += _dot((_silu(a) * u).astype(BF16), wd_ref[...])

    @pl.when(j == pl.num_programs(1) - 1)
    def _():
        gate = mod_ref[0, mod_row + 2:mod_row + 3, :]
        z = alpha * x_ref[...] + gate * (MACARON_WEIGHT * acc_ref[...])
        o_ref[...] = _layernorm(z, g_ref[...], b_ref[...])


def _ffn(x, mod, wg, wu, wd, ln_g, ln_b, *, mod_row, n_rows, n_lat_rows, seq, alpha):
    d = x.shape[1]
    f = wg.shape[1]
    n_batch = mod.shape[0] - 1
    tm = _pick(n_rows, (512, 256))
    tf = _pick(f, (512, 256, 128))
    vmem = (4 * tm * d * 4 + tm * d * 4 + tm * d * 2 + 2 * 3 * d * tf * 2
            + 8 * tm * tf * 4 + 2 * 16 * d * 4)
    idx = functools.partial(_mod_block_index, tm=tm, n_lat_rows=n_lat_rows, seq=seq, n_batch=n_batch)
    return pl.pallas_call(
        functools.partial(_ffn_kernel, mod_row=mod_row, alpha=alpha),
        out_shape=jax.ShapeDtypeStruct((n_rows, d), F32),
        grid=(n_rows // tm, f // tf),
        in_specs=[pl.BlockSpec((tm, d), lambda i, j: (i, 0)),
                  pl.BlockSpec((1, N_MOD, d), lambda i, j: (idx(i), 0, 0)),
                  pl.BlockSpec((d, tf), lambda i, j: (0, j)),
                  pl.BlockSpec((d, tf), lambda i, j: (0, j)),
                  pl.BlockSpec((tf, d), lambda i, j: (j, 0)),
                  pl.BlockSpec((1, d), lambda i, j: (0, 0)),
                  pl.BlockSpec((1, d), lambda i, j: (0, 0))],
        out_specs=pl.BlockSpec((tm, d), lambda i, j: (i, 0)),
        scratch_shapes=[pltpu.VMEM((tm, d), BF16), pltpu.VMEM((tm, d), F32)],
        compiler_params=_cparams(2, vmem),
        name="ffn_sublayer",
    )(x, mod, wg, wu, wd, ln_g, ln_b)


def _inproj_kernel(x_ref, mod_ref, w_ref, wex_ref, p_ref, ex_ref, h_ref, *, mod_row):
    j = pl.program_id(1)

    @pl.when(j == 0)
    def _():
        shift = mod_ref[0, mod_row:mod_row + 1, :]
        scale = mod_ref[0, mod_row + 1:mod_row + 2, :]
        h = (x_ref[...] * (1.0 + scale) + shift).astype(BF16)
        h_ref[...] = h
        ex_ref[...] = _dot(h, wex_ref[...])

    p_ref[...] = _dot(h_ref[...], w_ref[...])


def _inproj(x, mod, w_in, w_ex, *, mod_row, n_lat_rows, seq):
    n_rows, d = x.shape
    n = w_in.shape[1]
    n_batch = mod.shape[0] - 1
    tm = _pick(n_rows, (512, 256))
    tn = _pick(n, (512, 256, 128))
    vmem = (2 * tm * d * 4 + tm * d * 2 + 2 * d * tn * 2 + 2 * d * EX_WIDTH * 2
            + 3 * tm * tn * 4 + 2 * tm * EX_WIDTH * 4 + 2 * 16 * d * 4)
    idx = functools.partial(_mod_block_index, tm=tm, n_lat_rows=n_lat_rows, seq=seq, n_batch=n_batch)
    return pl.pallas_call(
        functools.partial(_inproj_kernel, mod_row=mod_row),
        out_shape=(jax.ShapeDtypeStruct((n_rows, n), F32),
                   jax.ShapeDtypeStruct((n_rows, EX_WIDTH), F32)),
        grid=(n_rows // tm, n // tn),
        in_specs=[pl.BlockSpec((tm, d), lambda i, j: (i, 0)),
                  pl.BlockSpec((1, N_MOD, d), lambda i, j: (idx(i), 0, 0)),
                  pl.BlockSpec((d, tn), lambda i, j: (0, j)),
                  pl.BlockSpec((d, EX_WIDTH), lambda i, j: (0, 0))],
        out_specs=(pl.BlockSpec((tm, tn), lambda i, j: (i, j)),
                   pl.BlockSpec((tm, EX_WIDTH), lambda i, j: (i, 0))),
        scratch_shapes=[pltpu.VMEM((tm, d), BF16)],
        compiler_params=_cparams(2, vmem),
        name="mixer_in_proj",
    )(x, mod, w_in, w_ex)


def _prep_kernel(dn_ref, prev_ref, next_ref, conv_ref, aq0_ref, aq1_ref, akv_ref, qg_ref, kg_ref,
                 cos_ref, sin_ref, dn_o, aq_o, akv_o, *, n_lat_blocks, blocks_per_seq):
    i = pl.program_id(0)
    is_lat = i < n_lat_blocks
    pos = i % blocks_per_seq
    has_prev = jnp.logical_and(is_lat, pos != 0)
    has_next = jnp.logical_and(is_lat, pos != blocks_per_seq - 1)

    z = dn_ref[...]
    rows = z.shape[0]
    ridx = lax.broadcasted_iota(jnp.int32, z.shape, 0)
    halo_p = jnp.where(has_prev, prev_ref[7:8, :], 0.0)
    halo_n = jnp.where(has_next, next_ref[0:1, :], 0.0)
    z_prev = jnp.where(ridx == 0, halo_p, pltpu.roll(z, 1, 0))
    z_next = jnp.where(ridx == rows - 1, halo_n, pltpu.roll(z, rows - 1, 0))
    y = _silu(z_prev * conv_ref[0:1, :] + z * conv_ref[1:2, :] + z_next * conv_ref[2:3, :])
    for hh in range(2 * DN_HEADS):
        sl = slice(hh * DN_DK, (hh + 1) * DN_DK)
        yh = y[:, sl]
        yn = yh * lax.rsqrt(jnp.sum(yh * yh, axis=-1, keepdims=True) + NORM_EPS)
        if hh < DN_HEADS:
            yn = yn * (DN_DK ** -0.5)
        dn_o[:, sl] = yn
    dn_o[:, 2 * DN_QK_W:] = y[:, 2 * DN_QK_W:]

    lane = lax.broadcasted_iota(jnp.int32, (rows, ATT_HD), 1)
    first_half_of_pair = (lane // (ATT_HD // 4)) % 2 == 0
    cos = cos_ref[...]
    sin = sin_ref[...]

    def rope(xh):
        swapped = jnp.where(first_half_of_pair,
                            pltpu.roll(xh, ATT_HD - ATT_HD // 4, 1), pltpu.roll(xh, ATT_HD // 4, 1))
        return jnp.where(is_lat, xh * cos + swapped * sin, xh)

    for hh in range(ATT_HEADS):
        src = aq0_ref if hh < ATT_HEADS // 2 else aq1_ref
        off = (hh % (ATT_HEADS // 2)) * ATT_HD
        qh = rope(_head_rmsnorm(src[:, off:off + ATT_HD], qg_ref[...])) * (ATT_HD ** -0.5)
        aq_o[:, hh * ATT_HD:(hh + 1) * ATT_HD] = qh.astype(BF16)
    for hh in range(ATT_KV_HEADS):
        sl = slice(hh * ATT_HD, (hh + 1) * ATT_HD)
        akv_o[:, sl] = rope(_head_rmsnorm(akv_ref[:, sl], kg_ref[...])).astype(BF16)
    akv_o[:, ATT_KV_W:] = akv_ref[:, ATT_KV_W:].astype(BF16)


def _prep(p, conv_w, q_norm_g, k_norm_g, rope_cos, rope_sin, *, n_lat_rows, seq):
    n_rows = p.shape[0]
    tm = SEQ_BLOCK
    n_blocks = n_rows // tm
    n_lat_blocks = n_lat_rows // tm
    bps = seq // tm
    dn_w = 2 * DN_QK_W + DN_V_W
    dn_cb = OFF_DN // dn_w
    assert OFF_DN % dn_w == 0 and OFF_ATT % 512 == 0
    att_cb = OFF_ATT // 512
    halo_per_block = tm // 8
    n_halo = n_rows // 8
    vmem = (2 * 2 * tm * dn_w * 4 + 2 * 3 * tm * 512 * 4 + 2 * tm * 1536 * 2
            + 10 * tm * dn_w * 4 + 4 * tm * LANES * 4)
    return pl.pallas_call(
        functools.partial(_prep_kernel, n_lat_blocks=n_lat_blocks, blocks_per_seq=bps),
        out_shape=(jax.ShapeDtypeStruct((n_rows, dn_w), F32),
                   jax.ShapeDtypeStruct((n_rows, ATT_Q_W), BF16),
                   jax.ShapeDtypeStruct((n_rows, 2 * ATT_KV_W), BF16)),
        grid=(n_blocks,),
        in_specs=[pl.BlockSpec((tm, dn_w), lambda i: (i, dn_cb)),
                  pl.BlockSpec((8, dn_w), lambda i: (jnp.maximum(i * halo_per_block - 1, 0), dn_cb)),
                  pl.BlockSpec((8, dn_w), lambda i: (jnp.minimum((i + 1) * halo_per_block, n_halo - 1), dn_cb)),
                  pl.BlockSpec((DN_CONV, dn_w), lambda i: (0, 0)),
                  pl.BlockSpec((tm, 512), lambda i: (i, att_cb)),
                  pl.BlockSpec((tm, 512), lambda i: (i, att_cb + 1)),
                  pl.BlockSpec((tm, 512), lambda i: (i, att_cb + 2)),
                  pl.BlockSpec((1, ATT_HD), lambda i: (0, 0)),
                  pl.BlockSpec((1, ATT_HD), lambda i: (0, 0)),
                  pl.BlockSpec((tm, ATT_HD), lambda i: (jnp.where(i < n_lat_blocks, i % bps, 0), 0)),
                  pl.BlockSpec((tm, ATT_HD), lambda i: (jnp.where(i < n_lat_blocks, i % bps, 0), 0))],
        out_specs=(pl.BlockSpec((tm, dn_w), lambda i: (i, 0)),
                   pl.BlockSpec((tm, ATT_Q_W), lambda i: (i, 0)),
                   pl.BlockSpec((tm, 2 * ATT_KV_W), lambda i: (i, 0))),
        compiler_params=_cparams(1, vmem),
        name="mixer_prep",
    )(p, p, p, conv_w, p, p, p, q_norm_g, k_norm_g, rope_cos, rope_sin)


def _seq_block(b, s, *, rev, n_lat_blocks, blocks_per_seq):
    j = (blocks_per_seq - s) if rev else (s - 1)
    return jnp.where(s == 0, n_lat_blocks + b, b * blocks_per_seq + j)


def _tri_masks(rev):
    row = lax.broadcasted_iota(jnp.int32, (CHUNK, CHUNK), 0)
    col = lax.broadcasted_iota(jnp.int32, (CHUNK, CHUNK), 1)
    incl = (row <= col) if rev else (row >= col)
    strict = (row < col) if rev else (row > col)
    return incl, strict


def _gla_kernel(qf, kf, vf, ef, qb, kb, vb, eb, wa2_ref, ba_ref, of_ref, ob_ref, st_ref):
    @pl.when(pl.program_id(1) == 0)
    def _():
        st_ref[...] = jnp.zeros(st_ref.shape, F32)

    for d, (q_ref, k_ref, v_ref, e_ref, o_ref) in enumerate(((qf, kf, vf, ef, of_ref),
                                                              (qb, kb, vb, eb, ob_ref))):
        rev = d == 1
        incl, _ = _tri_masks(rev)
        tri = jnp.where(incl, 1.0, 0.0).astype(BF16)
        low_rank = e_ref[:, d * GLA_RANK:(d + 1) * GLA_RANK].astype(BF16)
        logit = _dot(low_rank, wa2_ref[d].astype(BF16)) + ba_ref[d]
        log_a = -_softplus(-logit) / GLA_TAU
        chunks = range(CHUNKS_PER_BLOCK - 1, -1, -1) if rev else range(CHUNKS_PER_BLOCK)
        for c in chunks:
            sl = slice(c * CHUNK, (c + 1) * CHUNK)
            bcum = _cumsum_f32(tri, log_a[sl])
            blast = bcum[0:1] if rev else bcum[CHUNK - 1:CHUNK]
            q_in = q_ref[sl, :] * (GLA_DK ** -0.5) * jnp.exp(bcum)
            kc = k_ref[sl, :]
            k_in = kc * jnp.exp(-bcum)
            k_end = kc * jnp.exp(blast - bcum)
            decay = jnp.exp(blast)
            st = st_ref[d]
            outs, new_st = [], []
            for h in range(GLA_HEADS):
                hs = slice(h * GLA_DK, (h + 1) * GLA_DK)
                qh = q_in[:, hs].astype(BF16)
                attn = jnp.where(incl, _dot_nt(qh, k_in[:, hs].astype(BF16)), 0.0)
                vh = v_ref[sl, h * GLA_DV:(h + 1) * GLA_DV].astype(BF16)
                sth = st[:, hs]
                outs.append(_dot(attn.astype(BF16), vh) + _dot_nt(qh, sth.astype(BF16)))
                new_st.append(sth * decay[:, hs] + _dot_tn(vh, k_end[:, hs].astype(BF16)))
            o_ref[sl, :] = jnp.concatenate(outs, axis=1)
            st_ref[d] = jnp.concatenate(new_st, axis=1)


def _gla(p, ex, wa2, ba, *, n_batch, n_lat_rows, seq):
    n_rows = p.shape[0]
    tm = SEQ_BLOCK
    bps = seq // tm
    nlb = n_lat_rows // tm
    fwd = functools.partial(_seq_block, rev=False, n_lat_blocks=nlb, blocks_per_seq=bps)
    bwd = functools.partial(_seq_block, rev=True, n_lat_blocks=nlb, blocks_per_seq=bps)
    v_cb = OFF_GLA_V // GLA_V_W

    def specs(blk):
        return [pl.BlockSpec((tm, GLA_QK_W), lambda b, s: (blk(b, s), 0)),
                pl.BlockSpec((tm, GLA_QK_W), lambda b, s: (blk(b, s), 1)),
                pl.BlockSpec((tm, GLA_V_W), lambda b, s: (blk(b, s), v_cb)),
                pl.BlockSpec((tm, EX_WIDTH), lambda b, s: (blk(b, s), 0))]

    vmem = 2 * 2 * tm * (2 * GLA_QK_W + GLA_V_W + EX_WIDTH) * 4 + 2 * 2 * tm * GLA_V_W * 4 + 24 * tm * GLA_QK_W * 4
    out = jax.ShapeDtypeStruct((n_rows, GLA_V_W), F32)
    return pl.pallas_call(
        _gla_kernel,
        out_shape=(out, out),
        grid=(n_batch, bps + 1),
        in_specs=specs(fwd) + specs(bwd) + [
            pl.BlockSpec((2, GLA_RANK, GLA_QK_W), lambda b, s: (0, 0, 0)),
            pl.BlockSpec((2, 1, GLA_QK_W), lambda b, s: (0, 0, 0))],
        out_specs=(pl.BlockSpec((tm, GLA_V_W), lambda b, s: (fwd(b, s), 0)),
                   pl.BlockSpec((tm, GLA_V_W), lambda b, s: (bwd(b, s), 0))),
        scratch_shapes=[pltpu.VMEM((2, GLA_DV, GLA_QK_W), F32)],
        compiler_params=_cparams(2, vmem),
        name="gla_scan",
    )(p, p, p, ex, p, p, p, ex, wa2, ba)


def _unit_triangular_inverse(a):
    row = lax.broadcasted_iota(jnp.int32, (CHUNK, CHUNK), 0)
    col = lax.broadcasted_iota(jnp.int32, (CHUNK, CHUNK), 1)
    eye = jnp.where(row == col, 1.0, 0.0)
    n = -a
    t = eye + n
    p = _dot(n.astype(BF16), n.astype(BF16))
    for _ in range(int(math.log2(CHUNK)) - 2):
        pb = p.astype(BF16)
        both = _dot(jnp.concatenate([t, p], axis=0).astype(BF16), pb)
        t = t + both[:CHUNK]
        p = both[CHUNK:]
    return t + _dot(t.astype(BF16), p.astype(BF16))


def _dn_kernel(xf, ef, xb, eb, alog_ref, dtb_ref, of_ref, ob_ref, st_ref):
    @pl.when(pl.program_id(1) == 0)
    def _():
        st_ref[...] = jnp.zeros(st_ref.shape, F32)

    for d, (x_ref, e_ref, o_ref) in enumerate(((xf, ef, of_ref), (xb, eb, ob_ref))):
        rev = d == 1
        incl, strict = _tri_masks(rev)
        tri = jnp.where(incl, 1.0, 0.0).astype(BF16)
        a_in = e_ref[:, EX_DN + 2 * DN_HEADS * d:EX_DN + 2 * DN_HEADS * d + DN_HEADS]
        b_in = e_ref[:, EX_DN + 2 * DN_HEADS * d + DN_HEADS:EX_DN + 2 * DN_HEADS * (d + 1)]
        g_all = -jnp.exp(alog_ref[d]) * _softplus(a_in + dtb_ref[d])
        beta_all = _sigmoid(b_in)
        chunks = range(CHUNKS_PER_BLOCK - 1, -1, -1) if rev else range(CHUNKS_PER_BLOCK)
        for c in chunks:
            sl = slice(c * CHUNK, (c + 1) * CHUNK)
            gcum_all = _cumsum_f32(tri, g_all[sl])
            outs = []
            for h in range(DN_HEADS):
                q = x_ref[sl, h * DN_DK:(h + 1) * DN_DK]
                k = x_ref[sl, DN_QK_W + h * DN_DK:DN_QK_W + (h + 1) * DN_DK]
                v = x_ref[sl, 2 * DN_QK_W + h * DN_DV:2 * DN_QK_W + (h + 1) * DN_DV]
                beta = beta_all[sl, h:h + 1]
                gcum = gcum_all[:, h:h + 1]
                glast = gcum[0:1] if rev else gcum[CHUNK - 1:CHUNK]
                gmat = jnp.broadcast_to(gcum, (CHUNK, CHUNK))
                decay = jnp.exp(gmat - gmat.T)
                kb = k * beta
                kbf = k.astype(BF16)
                qk = _dot_nt(jnp.concatenate([q, kb], axis=0).astype(BF16), kbf)
                attn = jnp.where(incl, qk[:CHUNK] * decay, 0.0)
                tinv = _unit_triangular_inverse(jnp.where(strict, qk[CHUNK:] * decay, 0.0))
                egc = jnp.exp(gcum)
                uw = _dot(tinv.astype(BF16), jnp.concatenate([v * beta, kb * egc], axis=1).astype(BF16))
                u, w = uw[:, :DN_DV], uw[:, DN_DV:]
                state = st_ref[d, h]
                wq = _dot(jnp.concatenate([w, q * egc], axis=0).astype(BF16), state.astype(BF16))
                v_new = u - wq[:CHUNK]
                v_new_b = v_new.astype(BF16)
                outs.append(wq[CHUNK:] + _dot(attn.astype(BF16), v_new_b))
                k_end = (k * jnp.exp(glast - gcum)).astype(BF16)
                st_ref[d, h] = state * jnp.exp(glast) + _dot_tn(k_end, v_new_b)
            o_ref[sl, :] = jnp.concatenate(outs, axis=1)


def _dn(x, ex, a_log, dt_bias, *, n_batch, n_lat_rows, seq):
    n_rows, w = x.shape
    tm = SEQ_BLOCK
    bps = seq // tm
    nlb = n_lat_rows // tm
    fwd = functools.partial(_seq_block, rev=False, n_lat_blocks=nlb, blocks_per_seq=bps)
    bwd = functools.partial(_seq_block, rev=True, n_lat_blocks=nlb, blocks_per_seq=bps)

    def specs(blk):
        return [pl.BlockSpec((tm, w), lambda b, s: (blk(b, s), 0)),
                pl.BlockSpec((tm, EX_WIDTH), lambda b, s: (blk(b, s), 0))]

    vmem = 2 * 2 * tm * (w + EX_WIDTH) * 4 + 2 * 2 * tm * DN_V_W * 4 + 2 * DN_HEADS * DN_DK * DN_DV * 4 + 16 * tm * w * 4
    out = jax.ShapeDtypeStruct((n_rows, DN_V_W), F32)
    return pl.pallas_call(
        _dn_kernel,
        out_shape=(out, out),
        grid=(n_batch, bps + 1),
        in_specs=specs(fwd) + specs(bwd) + [
            pl.BlockSpec((2, 1, DN_HEADS), lambda b, s: (0, 0, 0)),
            pl.BlockSpec((2, 1, DN_HEADS), lambda b, s: (0, 0, 0))],
        out_specs=(pl.BlockSpec((tm, DN_V_W), lambda b, s: (fwd(b, s), 0)),
                   pl.BlockSpec((tm, DN_V_W), lambda b, s: (bwd(b, s), 0))),
        scratch_shapes=[pltpu.VMEM((2, DN_HEADS, DN_DK, DN_DV), F32)],
        compiler_params=_cparams(2, vmem),
        name="deltanet_scan",
    )(x, ex, x, ex, a_log, dt_bias)


def _attn_kernel(q_ref, kc_ref, vc_ref, kl_ref, vl_ref, o_ref, m_ref, l_ref, acc_ref,
                 *, with_ctx, tk):
    tq = q_ref.shape[0]
    q = jnp.concatenate([q_ref[:, g * ATT_HD:(g + 1) * ATT_HD] for g in range(ATT_GROUP)], axis=0)

    def start():
        m_ref[...] = jnp.full(m_ref.shape, -jnp.inf, F32)
        l_ref[...] = jnp.zeros(l_ref.shape, F32)
        acc_ref[...] = jnp.zeros(acc_ref.shape, F32)

    def step(k, v):
        s = _dot_nt(q, k)
        m_old = m_ref[...]
        m_new = jnp.maximum(m_old, jnp.max(s, axis=-1, keepdims=True))
        p = jnp.exp(s - m_new)
        alpha = jnp.exp(m_old - m_new)
        l_ref[...] = alpha * l_ref[...] + jnp.sum(p, axis=-1, keepdims=True)
        acc_ref[...] = alpha * acc_ref[...] + _dot(p.astype(BF16), v)
        m_ref[...] = m_new

    def finish():
        o = acc_ref[...] / l_ref[...]
        for g in range(ATT_GROUP):
            o_ref[:, g * ATT_HD:(g + 1) * ATT_HD] = o[g * tq:(g + 1) * tq]

    def latent_query():
        start()
        step(kc_ref[...], vc_ref[...])

        def body(c, carry):
            r = pl.multiple_of(c * tk, tk)
            step(kl_ref[pl.ds(r, tk), :], vl_ref[pl.ds(r, tk), :])
            return carry

        lax.fori_loop(0, kl_ref.shape[0] // tk, body, 0)
        finish()

    def context_query():
        start()
        step(kc_ref[...], vc_ref[...])
        finish()

    if with_ctx:
        is_ctx = pl.program_id(2) == 0
        pl.when(is_ctx)(context_query)
        pl.when(jnp.logical_not(is_ctx))(latent_query)
    else:
        latent_query()


def _attention(aq, akv, *, n_batch, n_lat_rows, seq, ctx_len, with_ctx):
    n_rows = aq.shape[0]
    tq = SEQ_BLOCK
    assert ctx_len == tq
    bps = seq // tq
    nlb = n_lat_rows // tq
    tk = _pick(seq, (512, 256))
    n_q = bps + (1 if with_ctx else 0)
    q_cw = ATT_GROUP * ATT_HD

    def q_blk(b, qi):
        if with_ctx:
            return jnp.where(qi == 0, nlb + b, b * bps + qi - 1)
        return b * bps + qi

    vmem = (2 * 2 * tq * q_cw * 4 + 2 * 2 * (seq + ctx_len) * ATT_HD * 2
            + ATT_GROUP * tq * (2 * LANES + ATT_HD) * 4 + 6 * ATT_GROUP * tq * tk * 4)
    return pl.pallas_call(
        functools.partial(_attn_kernel, with_ctx=with_ctx, tk=tk),
        out_shape=jax.ShapeDtypeStruct((n_rows if with_ctx else n_lat_rows, ATT_Q_W), F32),
        grid=(n_batch, ATT_KV_HEADS, n_q),
        in_specs=[pl.BlockSpec((tq, q_cw), lambda b, h, qi: (q_blk(b, qi), h)),
                  pl.BlockSpec((ctx_len, ATT_HD), lambda b, h, qi: (nlb + b, h)),
                  pl.BlockSpec((ctx_len, ATT_HD), lambda b, h, qi: (nlb + b, ATT_KV_HEADS + h)),
                  pl.BlockSpec((seq, ATT_HD), lambda b, h, qi: (b, h)),
                  pl.BlockSpec((seq, ATT_HD), lambda b, h, qi: (b, ATT_KV_HEADS + h))],
        out_specs=pl.BlockSpec((tq, q_cw), lambda b, h, qi: (q_blk(b, qi), h)),
        scratch_shapes=[pltpu.VMEM((ATT_GROUP * tq, 1), F32),
                        pltpu.VMEM((ATT_GROUP * tq, 1), F32),
                        pltpu.VMEM((ATT_GROUP * tq, ATT_HD), F32)],
        compiler_params=_cparams(3, vmem),
        name="gqa_attention",
    )(aq, akv, akv, akv, akv)


def _outproj_kernel(x_ref, mod_ref, gf_ref, gb_ref, gr_ref, df_ref, db_ref, dg_ref, at_ref,
                    gn_ref, dn_ref, w_ref, g_ref, b_ref, o_ref, *, mod_row, alpha):
    parts = []
    for f_ref, b_ref_, gate_ref, norm_ref, heads, dv in ((gf_ref, gb_ref, gr_ref, gn_ref, GLA_HEADS, GLA_DV),
                                                         (df_ref, db_ref, dg_ref, dn_ref, DN_HEADS, DN_DV)):
        for h in range(heads):
            sl = slice(h * dv, (h + 1) * dv)
            o = _head_rmsnorm(f_ref[:, sl] + b_ref_[:, sl], norm_ref[...]) * _silu(gate_ref[:, sl])
            parts.append(o.astype(BF16))
    parts.append(at_ref[...].astype(BF16))
    y = _dot(jnp.concatenate(parts, axis=1), w_ref[...])
    gate = mod_ref[0, mod_row:mod_row + 1, :]
    o_ref[...] = _layernorm(alpha * x_ref[...] + gate * y, g_ref[...], b_ref[...])


def _outproj(x, mod, gla_f, gla_b, p, dn_f, dn_b, att, gla_norm_g, dn_norm_g, w_out, ln_g, ln_b,
             *, mod_row, n_rows, n_lat_rows, seq, alpha):
    d = x.shape[1]
    n_batch = mod.shape[0] - 1
    tm = SEQ_BLOCK
    idx = functools.partial(_mod_block_index, tm=tm, n_lat_rows=n_lat_rows, seq=seq, n_batch=n_batch)
    row = lambda w, cb: pl.BlockSpec((tm, w), lambda i: (i, cb))
    vec = lambda w: pl.BlockSpec((1, w), lambda i: (0, 0))
    vmem = (2 * tm * (2 * d + 6 * GLA_V_W + ATT_Q_W) * 4 + 2 * MIX_WIDTH * d * 2
            + tm * MIX_WIDTH * 2 + 6 * tm * d * 4)
    return pl.pallas_call(
        functools.partial(_outproj_kernel, mod_row=mod_row, alpha=alpha),
        out_shape=jax.ShapeDtypeStruct((n_rows, d), F32),
        grid=(n_rows // tm,),
        in_specs=[row(d, 0),
                  pl.BlockSpec((1, N_MOD, d), lambda i: (idx(i), 0, 0)),
                  row(GLA_V_W, 0), row(GLA_V_W, 0), row(GLA_V_W, OFF_GLA_R // GLA_V_W),
                  row(DN_V_W, 0), row(DN_V_W, 0), row(DN_V_W, OFF_DN_GATE // DN_V_W),
                  row(ATT_Q_W, 0),
                  vec(GLA_DV), vec(DN_DV),
                  pl.BlockSpec((MIX_WIDTH, d), lambda i: (0, 0)),
                  vec(d), vec(d)],
        out_specs=row(d, 0),
        compiler_params=_cparams(1, vmem),
        name="mixer_out_proj",
    )(x, mod, gla_f, gla_b, p, dn_f, dn_b, p, att, gla_norm_g, dn_norm_g, w_out, ln_g, ln_b)


def _rope_tables(seq):
    t = jnp.arange(seq)
    pr = (t // GRID_W).astype(F32)
    pc = (t % GRID_W).astype(F32)
    axis_dim = ATT_HD // 2
    inv = ROPE_THETA ** (-jnp.arange(0, axis_dim, 2, dtype=F32) / axis_dim)
    ar = pr[:, None] * inv
    ac = pc[:, None] * inv
    cos = jnp.concatenate([jnp.cos(ar), jnp.cos(ar), jnp.cos(ac), jnp.cos(ac)], axis=1)
    sin = jnp.concatenate([-jnp.sin(ar), jnp.sin(ar), -jnp.sin(ac), jnp.sin(ac)], axis=1)
    return cos, sin


def kernel(x, c, ctx, c_ctx, w_ada, b_ada, ln_g, ln_b, w_ffn_gate, w_ffn_up, w_ffn_down, w_in,
           gla_wa1, gla_wa2, gla_ba, gla_norm_g, dn_conv, dn_wab, dn_a_log, dn_dt_bias, dn_norm_g,
           q_norm_g, k_norm_g, w_out):
    n_batch, seq, d = x.shape
    ctx_len = ctx.shape[1]
    depth = w_ada.shape[0]
    alpha = (2 * depth) ** 0.25
    n_lat = n_batch * seq
    n_all = n_lat + n_batch * ctx_len
    assert n_batch + 1 <= 8 and seq % SEQ_BLOCK == 0 and ctx_len == SEQ_BLOCK

    cond = jnp.zeros((8, d), F32).at[:n_batch].set(c).at[n_batch].set(c_ctx)
    mod_all = _ada_table(cond, w_ada, b_ada).reshape(depth, 8, N_MOD, d)
    rope_cos, rope_sin = _rope_tables(seq)
    xs = jnp.concatenate([x.reshape(n_lat, d), ctx.reshape(n_batch * ctx_len, d)], axis=0)

    for layer in range(depth):
        last = layer == depth - 1
        mod = mod_all[layer, :n_batch + 1]
        lg = ln_g[layer].reshape(3, 1, d)
        lb = ln_b[layer].reshape(3, 1, d)
        ffn_w = [(w_ffn_gate[layer, i].astype(BF16), w_ffn_up[layer, i].astype(BF16),
                  w_ffn_down[layer, i].astype(BF16)) for i in range(2)]
        w_ex = jnp.concatenate([gla_wa1[layer, 0], gla_wa1[layer, 1], dn_wab[layer, 0], dn_wab[layer, 1]], axis=1)
        w_ex = jnp.pad(w_ex, ((0, 0), (0, EX_WIDTH - w_ex.shape[1]))).astype(BF16)
        common = dict(n_lat_rows=n_lat, seq=seq)

        xs = _ffn(xs, mod, *ffn_w[0], lg[0], lb[0], mod_row=0, n_rows=n_all, alpha=alpha, **common)
        p, ex = _inproj(xs, mod, w_in[layer].astype(BF16), w_ex, mod_row=3, **common)
        dn_x, aq, akv = _prep(p, dn_conv[layer], q_norm_g[layer].reshape(1, ATT_HD),
                              k_norm_g[layer].reshape(1, ATT_HD), rope_cos, rope_sin, **common)
        gla_f, gla_b = _gla(p, ex, gla_wa2[layer], gla_ba[layer].reshape(2, 1, GLA_QK_W),
                            n_batch=n_batch, **common)
        dn_f, dn_b = _dn(dn_x, ex, dn_a_log[layer].reshape(2, 1, DN_HEADS),
                         dn_dt_bias[layer].reshape(2, 1, DN_HEADS), n_batch=n_batch, **common)
        att = _attention(aq, akv, n_batch=n_batch, ctx_len=ctx_len, with_ctx=not last, **common)
        n_out = n_lat if last else n_all
        xs = _outproj(xs, mod, gla_f, gla_b, p, dn_f, dn_b, att, gla_norm_g[layer].reshape(1, GLA_DV),
                      dn_norm_g[layer].reshape(1, DN_DV), w_out[layer].astype(BF16), lg[1], lb[1],
                      mod_row=5, n_rows=n_out, alpha=alpha, **common)
        xs = _ffn(xs, mod, *ffn_w[1], lg[2], lb[2], mod_row=6, n_rows=n_out, alpha=alpha, **common)
    return xs.reshape(n_batch, seq, d)
```

```python
import functools
import math

import jax
import jax.numpy as jnp
from jax import lax
from jax.experimental import pallas as pl
from jax.experimental.pallas import tpu as pltpu

F32 = jnp.float32
BF16 = jnp.bfloat16

N_MOD = 9
GRID_W = 64
MACARON_WEIGHT = 0.5
GLA_HEADS, GLA_DK, GLA_DV, GLA_RANK, GLA_TAU = 4, 64, 128, 16, 16.0
DN_HEADS, DN_DK, DN_DV, DN_CONV = 4, 128, 128, 3
ATT_HEADS, ATT_KV_HEADS, ATT_HD = 8, 2, 128
ATT_GROUP = ATT_HEADS // ATT_KV_HEADS
CHUNK = 64
ROPE_THETA = 10000.0
NORM_EPS = 1e-6
GLA_QK_W, GLA_V_W = GLA_HEADS * GLA_DK, GLA_HEADS * GLA_DV
DN_QK_W, DN_V_W = DN_HEADS * DN_DK, DN_HEADS * DN_DV
ATT_Q_W, ATT_KV_W = ATT_HEADS * ATT_HD, ATT_KV_HEADS * ATT_HD
IN_WIDTH = 2 * GLA_QK_W + 2 * GLA_V_W + 2 * DN_QK_W + 2 * DN_V_W + ATT_Q_W + 2 * ATT_KV_W
MIX_WIDTH = GLA_V_W + DN_V_W + ATT_Q_W
OFF_GLA_V, OFF_GLA_R = 2 * GLA_QK_W, 2 * GLA_QK_W + GLA_V_W
OFF_DN = OFF_GLA_R + GLA_V_W
OFF_DN_GATE = OFF_DN + 2 * DN_QK_W + DN_V_W
OFF_ATT = OFF_DN_GATE + DN_V_W
EX_WIDTH = 128
EX_DN = 2 * GLA_RANK

V7X_VMEM_BYTES = 64 * 1024 * 1024
LANES = 128
SEQ_BLOCK = 256
CHUNKS_PER_BLOCK = SEQ_BLOCK // CHUNK


def _cparams(n_axes, vmem_bytes):
    limit = int(min(max(vmem_bytes, 16 * 1024 * 1024), V7X_VMEM_BYTES - 8 * 1024 * 1024))
    return pltpu.CompilerParams(dimension_semantics=("arbitrary",) * n_axes,
                                vmem_limit_bytes=limit)


def _dot(a, b):
    return jnp.dot(a, b, preferred_element_type=F32)


def _dot_nt(a, b):
    return lax.dot_general(a, b, (((1,), (1,)), ((), ())), preferred_element_type=F32)


def _dot_tn(a, b):
    return lax.dot_general(a, b, (((0,), (0,)), ((), ())), preferred_element_type=F32)


def _sigmoid(x):
    return 1.0 / (1.0 + jnp.exp(-x))


def _silu(x):
    return x * _sigmoid(x)


def _softplus(x):
    return jnp.maximum(x, 0.0) + jnp.log(1.0 + jnp.exp(-jnp.abs(x)))


def _layernorm(z, g, b):
    mu = jnp.mean(z, axis=-1, keepdims=True)
    zc = z - mu
    var = jnp.mean(zc * zc, axis=-1, keepdims=True)
    return zc * lax.rsqrt(var + NORM_EPS) * g + b


def _head_rmsnorm(x, g):
    return x * lax.rsqrt(jnp.mean(x * x, axis=-1, keepdims=True) + NORM_EPS) * g


def _cumsum_f32(tri, g):
    g1 = g.astype(BF16)
    r1 = g - g1.astype(F32)
    g2 = r1.astype(BF16)
    g3 = (r1 - g2.astype(F32)).astype(BF16)
    return _dot(tri, g1) + _dot(tri, g2) + _dot(tri, g3)


def _pick(n, candidates):
    for c in candidates:
        if n % c == 0:
            return c
    raise ValueError(f"no block size in {candidates} divides {n}")


def _mod_block_index(i, tm, n_lat_rows, seq, n_batch):
    assert seq % tm == 0 and n_lat_rows % tm == 0
    r = i * tm
    return jnp.where(r < n_lat_rows, r // seq, n_batch)


def _ada_kernel(c_ref, w_ref, b_ref, o_ref):
    s = _silu(c_ref[...]).astype(BF16)
    o_ref[0] = _dot(s, w_ref[0].astype(BF16)) + b_ref[0]


def _ada_table(cond, w_ada, b_ada):
    n_layer, d, n = w_ada.shape
    tn = _pick(n, (1024, 512, 256, 128))
    vmem = 2 * d * tn * 4 + d * tn * 2 + 4 * 8 * (d + 2 * tn) * 4
    return pl.pallas_call(
        _ada_kernel,
        out_shape=jax.ShapeDtypeStruct((n_layer, 8, n), F32),
        grid=(n_layer, n // tn),
        in_specs=[pl.BlockSpec((8, d), lambda l, j: (0, 0)),
                  pl.BlockSpec((1, d, tn), lambda l, j: (l, 0, j)),
                  pl.BlockSpec((1, 1, tn), lambda l, j: (l, 0, j))],
        out_specs=pl.BlockSpec((1, 8, tn), lambda l, j: (l, 0, j)),
        compiler_params=_cparams(2, vmem),
        name="ada_table",
    )(cond, w_ada, b_ada.reshape(n_layer, 1, n))


def _ffn_kernel(x_ref, mod_ref, wg_ref, wu_ref, wd_ref, g_ref, b_ref, o_ref, h_ref, acc_ref,
                *, mod_row, alpha):
    j = pl.program_id(1)

    @pl.when(j == 0)
    def _():
        shift = mod_ref[0, mod_row:mod_row + 1, :]
        scale = mod_ref[0, mod_row + 1:mod_row + 2, :]
        h_ref[...] = (x_ref[...] * (1.0 + scale) + shift).astype(BF16)
        acc_ref[...] = jnp.zeros(acc_ref.shape, F32)

    h = h_ref[...]
    a = _dot(h, wg_ref[...])
    u = _dot(h, wu_ref[...])
    acc_ref[...] += _dot((_silu(a) * u).astype(BF16), wd_ref[...])

    @pl.when(j == pl.num_programs(1) - 1)
    def _():
        gate = mod_ref[0, mod_row + 2:mod_row + 3, :]
        z = alpha * x_ref[...] + gate * (MACARON_WEIGHT * acc_ref[...])
        o_ref[...] = _layernorm(z, g_ref[...], b_ref[...])


def _ffn(x, mod, wg, wu, wd, ln_g, ln_b, *, mod_row, n_rows, n_lat_rows, seq, alpha):
    d = x.shape[1]
    f = wg.shape[1]
    n_batch = mod.shape[0] - 1
    tm = _pick(n_rows, (512, 256))
    tf = _pick(f, (512, 256, 128))
    vmem = (4 * tm * d * 4 + tm * d * 4 + tm * d * 2 + 2 * 3 * d * tf * 2
            + 8 * tm * tf * 4 + 2 * 16 * d * 4)
    idx = functools.partial(_mod_block_index, tm=tm, n_lat_rows=n_lat_rows, seq=seq, n_batch=n_batch)
    return pl.pallas_call(
        functools.partial(_ffn_kernel, mod_row=mod_row, alpha=alpha),
        out_shape=jax.ShapeDtypeStruct((n_rows, d), F32),
        grid=(n_rows // tm, f // tf),
        in_specs=[pl.BlockSpec((tm, d), lambda i, j: (i, 0)),
                  pl.BlockSpec((1, N_MOD, d), lambda i, j: (idx(i), 0, 0)),
                  pl.BlockSpec((d, tf), lambda i, j: (0, j)),
                  pl.BlockSpec((d, tf), lambda i, j: (0, j)),
                  pl.BlockSpec((tf, d), lambda i, j: (j, 0)),
                  pl.BlockSpec((1, d), lambda i, j: (0, 0)),
                  pl.BlockSpec((1, d), lambda i, j: (0, 0))],
        out_specs=pl.BlockSpec((tm, d), lambda i, j: (i, 0)),
        scratch_shapes=[pltpu.VMEM((tm, d), BF16), pltpu.VMEM((tm, d), F32)],
        compiler_params=_cparams(2, vmem),
        name="ffn_sublayer",
    )(x, mod, wg, wu, wd, ln_g, ln_b)


def _inproj_kernel(x_ref, mod_ref, w_ref, wex_ref, p_ref, ex_ref, h_ref, *, mod_row):
    j = pl.program_id(1)

    @pl.when(j == 0)
    def _():
        shift = mod_ref[0, mod_row:mod_row + 1, :]
        scale = mod_ref[0, mod_row + 1:mod_row + 2, :]
        h = (x_ref[...] * (1.0 + scale) + shift).astype(BF16)
        h_ref[...] = h
        ex_ref[...] = _dot(h, wex_ref[...])

    p_ref[...] = _dot(h_ref[...], w_ref[...])


def _inproj(x, mod, w_in, w_ex, *, mod_row, n_lat_rows, seq):
    n_rows, d = x.shape
    n = w_in.shape[1]
    n_batch = mod.shape[0] - 1
    tm = _pick(n_rows, (512, 256))
    tn = _pick(n, (512, 256, 128))
    vmem = (2 * tm * d * 4 + tm * d * 2 + 2 * d * tn * 2 + 2 * d * EX_WIDTH * 2
            + 3 * tm * tn * 4 + 2 * tm * EX_WIDTH * 4 + 2 * 16 * d * 4)
    idx = functools.partial(_mod_block_index, tm=tm, n_lat_rows=n_lat_rows, seq=seq, n_batch=n_batch)
    return pl.pallas_call(
        functools.partial(_inproj_kernel, mod_row=mod_row),
        out_shape=(jax.ShapeDtypeStruct((n_rows, n), F32),
                   jax.ShapeDtypeStruct((n_rows, EX_WIDTH), F32)),
        grid=(n_rows // tm, n // tn),
        in_specs=[pl.BlockSpec((tm, d), lambda i, j: (i, 0)),
                  pl.BlockSpec((1, N_MOD, d), lambda i, j: (idx(i), 0, 0)),
                  pl.BlockSpec((d, tn), lambda i, j: (0, j)),
                  pl.BlockSpec((d, EX_WIDTH), lambda i, j: (0, 0))],
        out_specs=(pl.BlockSpec((tm, tn), lambda i, j: (i, j)),
                   pl.BlockSpec((tm, EX_WIDTH), lambda i, j: (i, 0))),
        scratch_shapes=[pltpu.VMEM((tm, d), BF16)],
        compiler_params=_cparams(2, vmem),
        name="mixer_in_proj",
    )(x, mod, w_in, w_ex)


def _prep_kernel(dn_ref, prev_ref, next_ref, conv_ref, aq0_ref, aq1_ref, akv_ref, qg_ref, kg_ref,
                 cos_ref, sin_ref, dn_o, aq_o, ak_o, avt_o, *, n_lat_blocks, blocks_per_seq):
    i = pl.program_id(0)
    is_lat = i < n_lat_blocks
    pos = i % blocks_per_seq
    has_prev = jnp.logical_and(is_lat, pos != 0)
    has_next = jnp.logical_and(is_lat, pos != blocks_per_seq - 1)

    z = dn_ref[...]
    rows = z.shape[0]
    ridx = lax.broadcasted_iota(jnp.int32, z.shape, 0)
    halo_p = jnp.where(has_prev, prev_ref[7:8, :], 0.0)
    halo_n = jnp.where(has_next, next_ref[0:1, :], 0.0)
    z_prev = jnp.where(ridx == 0, halo_p, pltpu.roll(z, 1, 0))
    z_next = jnp.where(ridx == rows - 1, halo_n, pltpu.roll(z, rows - 1, 0))
    y = _silu(z_prev * conv_ref[0:1, :] + z * conv_ref[1:2, :] + z_next * conv_ref[2:3, :])
    for hh in range(2 * DN_HEADS):
        sl = slice(hh * DN_DK, (hh + 1) * DN_DK)
        yh = y[:, sl]
        yn = yh * lax.rsqrt(jnp.sum(yh * yh, axis=-1, keepdims=True) + NORM_EPS)
        if hh < DN_HEADS:
            yn = yn * (DN_DK ** -0.5)
        dn_o[:, sl] = yn
    dn_o[:, 2 * DN_QK_W:] = y[:, 2 * DN_QK_W:]

    lane = lax.broadcasted_iota(jnp.int32, (rows, ATT_HD), 1)
    first_half_of_pair = (lane // (ATT_HD // 4)) % 2 == 0
    cos = cos_ref[...]
    sin = sin_ref[...]

    def rope(xh):
        swapped = jnp.where(first_half_of_pair,
                            pltpu.roll(xh, ATT_HD - ATT_HD // 4, 1), pltpu.roll(xh, ATT_HD // 4, 1))
        return jnp.where(is_lat, xh * cos + swapped * sin, xh)

    for hh in range(ATT_HEADS):
        src = aq0_ref if hh < ATT_HEADS // 2 else aq1_ref
        off = (hh % (ATT_HEADS // 2)) * ATT_HD
        qh = rope(_head_rmsnorm(src[:, off:off + ATT_HD], qg_ref[...])) * (ATT_HD ** -0.5)
        aq_o[:, hh * ATT_HD:(hh + 1) * ATT_HD] = qh.astype(BF16)
    for hh in range(ATT_KV_HEADS):
        sl = slice(hh * ATT_HD, (hh + 1) * ATT_HD)
        ak_o[:, sl] = rope(_head_rmsnorm(akv_ref[:, sl], kg_ref[...])).astype(BF16)
        avt_o[sl, :] = akv_ref[:, ATT_KV_W + hh * ATT_HD:ATT_KV_W + (hh + 1) * ATT_HD].T.astype(BF16)


def _prep(p, conv_w, q_norm_g, k_norm_g, rope_cos, rope_sin, *, n_lat_rows, seq):
    n_rows = p.shape[0]
    tm = SEQ_BLOCK
    n_blocks = n_rows // tm
    n_lat_blocks = n_lat_rows // tm
    bps = seq // tm
    dn_w = 2 * DN_QK_W + DN_V_W
    dn_cb = OFF_DN // dn_w
    assert OFF_DN % dn_w == 0 and OFF_ATT % 512 == 0
    att_cb = OFF_ATT // 512
    halo_per_block = tm // 8
    n_halo = n_rows // 8
    vmem = (2 * 2 * tm * dn_w * 4 + 2 * 3 * tm * 512 * 4 + 2 * tm * 1536 * 2
            + 10 * tm * dn_w * 4 + 4 * tm * LANES * 4)
    return pl.pallas_call(
        functools.partial(_prep_kernel, n_lat_blocks=n_lat_blocks, blocks_per_seq=bps),
        out_shape=(jax.ShapeDtypeStruct((n_rows, dn_w), F32),
                   jax.ShapeDtypeStruct((n_rows, ATT_Q_W), BF16),
                   jax.ShapeDtypeStruct((n_rows, ATT_KV_W), BF16),
                   jax.ShapeDtypeStruct((ATT_KV_W, n_rows), BF16)),
        grid=(n_blocks,),
        in_specs=[pl.BlockSpec((tm, dn_w), lambda i: (i, dn_cb)),
                  pl.BlockSpec((8, dn_w), lambda i: (jnp.maximum(i * halo_per_block - 1, 0), dn_cb)),
                  pl.BlockSpec((8, dn_w), lambda i: (jnp.minimum((i + 1) * halo_per_block, n_halo - 1), dn_cb)),
                  pl.BlockSpec((DN_CONV, dn_w), lambda i: (0, 0)),
                  pl.BlockSpec((tm, 512), lambda i: (i, att_cb)),
                  pl.BlockSpec((tm, 512), lambda i: (i, att_cb + 1)),
                  pl.BlockSpec((tm, 512), lambda i: (i, att_cb + 2)),
                  pl.BlockSpec((1, ATT_HD), lambda i: (0, 0)),
                  pl.BlockSpec((1, ATT_HD), lambda i: (0, 0)),
                  pl.BlockSpec((tm, ATT_HD), lambda i: (jnp.where(i < n_lat_blocks, i % bps, 0), 0)),
                  pl.BlockSpec((tm, ATT_HD), lambda i: (jnp.where(i < n_lat_blocks, i % bps, 0), 0))],
        out_specs=(pl.BlockSpec((tm, dn_w), lambda i: (i, 0)),
                   pl.BlockSpec((tm, ATT_Q_W), lambda i: (i, 0)),
                   pl.BlockSpec((tm, ATT_KV_W), lambda i: (i, 0)),
                   pl.BlockSpec((ATT_KV_W, tm), lambda i: (0, i))),
        compiler_params=_cparams(1, vmem),
        name="mixer_prep",
    )(p, p, p, conv_w, p, p, p, q_norm_g, k_norm_g, rope_cos, rope_sin)


def _seq_block(b, s, *, rev, n_lat_blocks, blocks_per_seq):
    j = (blocks_per_seq - s) if rev else (s - 1)
    return jnp.where(s == 0, n_lat_blocks + b, b * blocks_per_seq + j)


def _tri_masks(rev):
    row = lax.broadcasted_iota(jnp.int32, (CHUNK, CHUNK), 0)
    col = lax.broadcasted_iota(jnp.int32, (CHUNK, CHUNK), 1)
    incl = (row <= col) if rev else (row >= col)
    strict = (row < col) if rev else (row > col)
    return incl, strict


def _gla_kernel(qf, kf, vf, ef, qb, kb, vb, eb, wa2_ref, ba_ref, of_ref, ob_ref, st_ref):
    @pl.when(pl.program_id(1) == 0)
    def _():
        st_ref[...] = jnp.zeros(st_ref.shape, F32)

    dirs = ((qf, kf, vf, ef, of_ref), (qb, kb, vb, eb, ob_ref))
    masks = [_tri_masks(rev)[0] for rev in (False, True)]
    tris = [jnp.where(m, 1.0, 0.0).astype(BF16) for m in masks]
    heads = [(d, h) for d in range(2) for h in range(GLA_HEADS)]

    def chunk_step(c, carry):
        prep = []
        for d, (q_ref, k_ref, v_ref, e_ref, o_ref) in enumerate(dirs):
            rev = d == 1
            cc = (CHUNKS_PER_BLOCK - 1 - c) if rev else c
            rows = pl.ds(pl.multiple_of(cc * CHUNK, CHUNK), CHUNK)
            low_rank = e_ref[rows, d * GLA_RANK:(d + 1) * GLA_RANK].astype(BF16)
            logit = _dot(low_rank, wa2_ref[d].astype(BF16)) + ba_ref[d]
            log_a = -_softplus(-logit) / GLA_TAU
            bcum = _cumsum_f32(tris[d], log_a)
            blast = bcum[0:1] if rev else bcum[CHUNK - 1:CHUNK]
            kc = k_ref[rows, :]
            prep.append(dict(
                rows=rows,
                q_in=(q_ref[rows, :] * (GLA_DK ** -0.5) * jnp.exp(bcum)).astype(BF16),
                k_in=(kc * jnp.exp(-bcum)).astype(BF16),
                k_end=(kc * jnp.exp(blast - bcum)).astype(BF16),
                decay=jnp.exp(blast),
                v=v_ref[rows, :].astype(BF16),
                st=st_ref[d]))
        hs = lambda h: slice(h * GLA_DK, (h + 1) * GLA_DK)
        vs = lambda h: slice(h * GLA_DV, (h + 1) * GLA_DV)
        attn = [jnp.where(masks[d], _dot_nt(prep[d]["q_in"][:, hs(h)], prep[d]["k_in"][:, hs(h)]), 0.0)
                .astype(BF16) for d, h in heads]
        inter = [_dot_nt(prep[d]["q_in"][:, hs(h)], prep[d]["st"][:, hs(h)].astype(BF16)) for d, h in heads]
        intra = [_dot(attn[i], prep[d]["v"][:, vs(h)]) for i, (d, h) in enumerate(heads)]
        upd = [_dot_tn(prep[d]["v"][:, vs(h)], prep[d]["k_end"][:, hs(h)]) for d, h in heads]
        for d, (_, _, _, _, o_ref) in enumerate(dirs):
            sel = [i for i, (dd, _) in enumerate(heads) if dd == d]
            o_ref[prep[d]["rows"], :] = jnp.concatenate([intra[i] + inter[i] for i in sel], axis=1)
            st_ref[d] = prep[d]["st"] * prep[d]["decay"] + jnp.concatenate([upd[i] for i in sel], axis=1)
        return carry

    lax.fori_loop(0, CHUNKS_PER_BLOCK, chunk_step, 0)


def _gla(p, ex, wa2, ba, *, n_batch, n_lat_rows, seq):
    n_rows = p.shape[0]
    tm = SEQ_BLOCK
    bps = seq // tm
    nlb = n_lat_rows // tm
    fwd = functools.partial(_seq_block, rev=False, n_lat_blocks=nlb, blocks_per_seq=bps)
    bwd = functools.partial(_seq_block, rev=True, n_lat_blocks=nlb, blocks_per_seq=bps)
    v_cb = OFF_GLA_V // GLA_V_W

    def specs(blk):
        return [pl.BlockSpec((tm, GLA_QK_W), lambda b, s: (blk(b, s), 0)),
                pl.BlockSpec((tm, GLA_QK_W), lambda b, s: (blk(b, s), 1)),
                pl.BlockSpec((tm, GLA_V_W), lambda b, s: (blk(b, s), v_cb)),
                pl.BlockSpec((tm, EX_WIDTH), lambda b, s: (blk(b, s), 0))]

    vmem = 2 * 2 * tm * (2 * GLA_QK_W + GLA_V_W + EX_WIDTH) * 4 + 2 * 2 * tm * GLA_V_W * 4 + 24 * tm * GLA_QK_W * 4
    out = jax.ShapeDtypeStruct((n_rows, GLA_V_W), F32)
    return pl.pallas_call(
        _gla_kernel,
        out_shape=(out, out),
        grid=(n_batch, bps + 1),
        in_specs=specs(fwd) + specs(bwd) + [
            pl.BlockSpec((2, GLA_RANK, GLA_QK_W), lambda b, s: (0, 0, 0)),
            pl.BlockSpec((2, 1, GLA_QK_W), lambda b, s: (0, 0, 0))],
        out_specs=(pl.BlockSpec((tm, GLA_V_W), lambda b, s: (fwd(b, s), 0)),
                   pl.BlockSpec((tm, GLA_V_W), lambda b, s: (bwd(b, s), 0))),
        scratch_shapes=[pltpu.VMEM((2, GLA_DV, GLA_QK_W), F32)],
        compiler_params=_cparams(2, vmem),
        name="gla_scan",
    )(p, p, p, ex, p, p, p, ex, wa2, ba)


def _dn_kernel(xf, ef, xb, eb, alog_ref, dtb_ref, of_ref, ob_ref, st_ref):
    @pl.when(pl.program_id(1) == 0)
    def _():
        st_ref[...] = jnp.zeros(st_ref.shape, F32)

    dirs = ((xf, ef, of_ref), (xb, eb, ob_ref))
    masks = [_tri_masks(rev) for rev in (False, True)]
    tris = [jnp.where(m[0], 1.0, 0.0).astype(BF16) for m in masks]
    row = lax.broadcasted_iota(jnp.int32, (CHUNK, CHUNK), 0)
    col = lax.broadcasted_iota(jnp.int32, (CHUNK, CHUNK), 1)
    eye = jnp.where(row == col, 1.0, 0.0)

    def chunk_step(c, carry):
        units = []
        for d, (x_ref, e_ref, o_ref) in enumerate(dirs):
            rev = d == 1
            cc = (CHUNKS_PER_BLOCK - 1 - c) if rev else c
            rows = pl.ds(pl.multiple_of(cc * CHUNK, CHUNK), CHUNK)
            lo = EX_DN + 2 * DN_HEADS * d
            a_in = e_ref[rows, lo:lo + DN_HEADS]
            b_in = e_ref[rows, lo + DN_HEADS:lo + 2 * DN_HEADS]
            g = -jnp.exp(alog_ref[d]) * _softplus(a_in + dtb_ref[d])
            beta_all = _sigmoid(b_in)
            gcum_all = _cumsum_f32(tris[d], g)
            for h in range(DN_HEADS):
                gcum = gcum_all[:, h:h + 1]
                glast = gcum[0:1] if rev else gcum[CHUNK - 1:CHUNK]
                gmat = jnp.broadcast_to(gcum, (CHUNK, CHUNK))
                units.append(dict(
                    d=d, h=h, rows=rows, beta=beta_all[:, h:h + 1], gcum=gcum, glast=glast,
                    decay=jnp.exp(gmat - gmat.T),
                    q=x_ref[rows, h * DN_DK:(h + 1) * DN_DK],
                    k=x_ref[rows, DN_QK_W + h * DN_DK:DN_QK_W + (h + 1) * DN_DK],
                    v=x_ref[rows, 2 * DN_QK_W + h * DN_DV:2 * DN_QK_W + (h + 1) * DN_DV]))
        for u in units:
            u["kb"] = u["k"] * u["beta"]
            u["qk"] = _dot_nt(jnp.concatenate([u["q"], u["kb"]], axis=0).astype(BF16), u["k"].astype(BF16))
        for u in units:
            incl, strict = masks[u["d"]]
            u["attn"] = jnp.where(incl, u["qk"][:CHUNK] * u["decay"], 0.0).astype(BF16)
            n = -jnp.where(strict, u["qk"][CHUNK:] * u["decay"], 0.0)
            u["t"] = eye + n
            u["n"] = n.astype(BF16)
        for u in units:
            u["p"] = _dot(u["n"], u["n"])
        for _ in range(int(math.log2(CHUNK)) - 2):
            for u in units:
                u["both"] = _dot(jnp.concatenate([u["t"], u["p"]], axis=0).astype(BF16), u["p"].astype(BF16))
            for u in units:
                u["t"] = u["t"] + u["both"][:CHUNK]
                u["p"] = u["both"][CHUNK:]
        for u in units:
            u["tp"] = _dot(u["t"].astype(BF16), u["p"].astype(BF16))
        for u in units:
            tinv = (u["t"] + u["tp"]).astype(BF16)
            egc = jnp.exp(u["gcum"])
            rhs = jnp.concatenate([u["v"] * u["beta"], u["kb"] * egc], axis=1).astype(BF16)
            u["uw"] = _dot(tinv, rhs)
            u["q_dec"] = u["q"] * egc
            u["k_end"] = (u["k"] * jnp.exp(u["glast"] - u["gcum"])).astype(BF16)
        for u in units:
            u["state"] = st_ref[u["d"], u["h"]]
            lhs = jnp.concatenate([u["uw"][:, DN_DV:], u["q_dec"]], axis=0).astype(BF16)
            u["wq"] = _dot(lhs, u["state"].astype(BF16))
        for u in units:
            u["v_new"] = (u["uw"][:, :DN_DV] - u["wq"][:CHUNK]).astype(BF16)
            u["o"] = u["wq"][CHUNK:] + _dot(u["attn"], u["v_new"])
        for u in units:
            st_ref[u["d"], u["h"]] = u["state"] * jnp.exp(u["glast"]) + _dot_tn(u["k_end"], u["v_new"])
        for d, (_, _, o_ref) in enumerate(dirs):
            sel = [u for u in units if u["d"] == d]
            o_ref[sel[0]["rows"], :] = jnp.concatenate([u["o"] for u in sel], axis=1)
        return carry

    lax.fori_loop(0, CHUNKS_PER_BLOCK, chunk_step, 0)


def _dn(x, ex, a_log, dt_bias, *, n_batch, n_lat_rows, seq):
    n_rows, w = x.shape
    tm = SEQ_BLOCK
    bps = seq // tm
    nlb = n_lat_rows // tm
    fwd = functools.partial(_seq_block, rev=False, n_lat_blocks=nlb, blocks_per_seq=bps)
    bwd = functools.partial(_seq_block, rev=True, n_lat_blocks=nlb, blocks_per_seq=bps)

    def specs(blk):
        return [pl.BlockSpec((tm, w), lambda b, s: (blk(b, s), 0)),
                pl.BlockSpec((tm, EX_WIDTH), lambda b, s: (blk(b, s), 0))]

    vmem = 2 * 2 * tm * (w + EX_WIDTH) * 4 + 2 * 2 * tm * DN_V_W * 4 + 2 * DN_HEADS * DN_DK * DN_DV * 4 + 16 * tm * w * 4
    out = jax.ShapeDtypeStruct((n_rows, DN_V_W), F32)
    return pl.pallas_call(
        _dn_kernel,
        out_shape=(out, out),
        grid=(n_batch, bps + 1),
        in_specs=specs(fwd) + specs(bwd) + [
            pl.BlockSpec((2, 1, DN_HEADS), lambda b, s: (0, 0, 0)),
            pl.BlockSpec((2, 1, DN_HEADS), lambda b, s: (0, 0, 0))],
        out_specs=(pl.BlockSpec((tm, DN_V_W), lambda b, s: (fwd(b, s), 0)),
                   pl.BlockSpec((tm, DN_V_W), lambda b, s: (bwd(b, s), 0))),
        scratch_shapes=[pltpu.VMEM((2, DN_HEADS, DN_DK, DN_DV), F32)],
        compiler_params=_cparams(2, vmem),
        name="deltanet_scan",
    )(x, ex, x, ex, a_log, dt_bias)


def _attn_kernel(q_ref, kc_ref, vtc_ref, kl_ref, vtl_ref, o_ref, m_ref, l_ref, acc_ref,
                 *, with_ctx, tk):
    tq = q_ref.shape[0]
    q = jnp.concatenate([q_ref[:, g * ATT_HD:(g + 1) * ATT_HD] for g in range(ATT_GROUP)], axis=0)

    def start():
        m_ref[...] = jnp.full(m_ref.shape, -jnp.inf, F32)
        l_ref[...] = jnp.zeros(l_ref.shape, F32)
        acc_ref[...] = jnp.zeros(acc_ref.shape, F32)

    def step(k, vt):
        s = _dot_nt(k, q)
        m_old = m_ref[...]
        m_new = jnp.maximum(m_old, jnp.max(s, axis=0, keepdims=True))
        p = jnp.exp(s - m_new)
        alpha = jnp.exp(m_old - m_new)
        l_ref[...] = alpha * l_ref[...] + jnp.sum(p, axis=0, keepdims=True)
        acc_ref[...] = alpha * acc_ref[...] + _dot(vt, p.astype(BF16))
        m_ref[...] = m_new

    def finish():
        o = acc_ref[...] / l_ref[...]
        for g in range(ATT_GROUP):
            o_ref[:, g * ATT_HD:(g + 1) * ATT_HD] = o[:, g * tq:(g + 1) * tq].T

    def latent_query():
        start()
        step(kc_ref[...], vtc_ref[...])

        def body(c, carry):
            r = pl.multiple_of(c * tk, tk)
            step(kl_ref[pl.ds(r, tk), :], vtl_ref[:, pl.ds(r, tk)])
            return carry

        lax.fori_loop(0, kl_ref.shape[0] // tk, body, 0)
        finish()

    def context_query():
        start()
        step(kc_ref[...], vtc_ref[...])
        finish()

    if with_ctx:
        is_ctx = pl.program_id(2) == 0
        pl.when(is_ctx)(context_query)
        pl.when(jnp.logical_not(is_ctx))(latent_query)
    else:
        latent_query()


def _attention(aq, ak, avt, *, n_batch, n_lat_rows, seq, ctx_len, with_ctx):
    n_rows = aq.shape[0]
    tq = SEQ_BLOCK
    assert ctx_len == tq
    bps = seq // tq
    nlb = n_lat_rows // tq
    tk = _pick(seq, (512, 256))
    n_q = bps + (1 if with_ctx else 0)
    q_cw = ATT_GROUP * ATT_HD

    def q_blk(b, qi):
        if with_ctx:
            return jnp.where(qi == 0, nlb + b, b * bps + qi - 1)
        return b * bps + qi

    vmem = (2 * tq * q_cw * (2 + 4) + 2 * 2 * (seq + ctx_len) * ATT_HD * 2
            + ATT_GROUP * tq * (2 * 8 + ATT_HD) * 4 + 6 * ATT_GROUP * tq * tk * 4)
    return pl.pallas_call(
        functools.partial(_attn_kernel, with_ctx=with_ctx, tk=tk),
        out_shape=jax.ShapeDtypeStruct((n_rows if with_ctx else n_lat_rows, ATT_Q_W), F32),
        grid=(n_batch, ATT_KV_HEADS, n_q),
        in_specs=[pl.BlockSpec((tq, q_cw), lambda b, h, qi: (q_blk(b, qi), h)),
                  pl.BlockSpec((ctx_len, ATT_HD), lambda b, h, qi: (nlb + b, h)),
                  pl.BlockSpec((ATT_HD, ctx_len), lambda b, h, qi: (h, nlb + b)),
                  pl.BlockSpec((seq, ATT_HD), lambda b, h, qi: (b, h)),
                  pl.BlockSpec((ATT_HD, seq), lambda b, h, qi: (h, b))],
        out_specs=pl.BlockSpec((tq, q_cw), lambda b, h, qi: (q_blk(b, qi), h)),
        scratch_shapes=[pltpu.VMEM((1, ATT_GROUP * tq), F32),
                        pltpu.VMEM((1, ATT_GROUP * tq), F32),
                        pltpu.VMEM((ATT_HD, ATT_GROUP * tq), F32)],
        compiler_params=_cparams(3, vmem),
        name="gqa_attention",
    )(aq, ak, avt, ak, avt)


def _outproj_kernel(x_ref, mod_ref, gf_ref, gb_ref, gr_ref, df_ref, db_ref, dg_ref, at_ref,
                    gn_ref, dn_ref, w_ref, g_ref, b_ref, o_ref, *, mod_row, alpha):
    parts = []
    for f_ref, b_ref_, gate_ref, norm_ref, heads, dv in ((gf_ref, gb_ref, gr_ref, gn_ref, GLA_HEADS, GLA_DV),
                                                         (df_ref, db_ref, dg_ref, dn_ref, DN_HEADS, DN_DV)):
        for h in range(heads):
            sl = slice(h * dv, (h + 1) * dv)
            o = _head_rmsnorm(f_ref[:, sl] + b_ref_[:, sl], norm_ref[...]) * _silu(gate_ref[:, sl])
            parts.append(o.astype(BF16))
    parts.append(at_ref[...].astype(BF16))
    y = _dot(jnp.concatenate(parts, axis=1), w_ref[...])
    gate = mod_ref[0, mod_row:mod_row + 1, :]
    o_ref[...] = _layernorm(alpha * x_ref[...] + gate * y, g_ref[...], b_ref[...])


def _outproj(x, mod, gla_f, gla_b, p, dn_f, dn_b, att, gla_norm_g, dn_norm_g, w_out, ln_g, ln_b,
             *, mod_row, n_rows, n_lat_rows, seq, alpha):
    d = x.shape[1]
    n_batch = mod.shape[0] - 1
    tm = SEQ_BLOCK
    idx = functools.partial(_mod_block_index, tm=tm, n_lat_rows=n_lat_rows, seq=seq, n_batch=n_batch)
    row = lambda w, cb: pl.BlockSpec((tm, w), lambda i: (i, cb))
    vec = lambda w: pl.BlockSpec((1, w), lambda i: (0, 0))
    vmem = (2 * tm * (2 * d + 6 * GLA_V_W + ATT_Q_W) * 4 + 2 * MIX_WIDTH * d * 2
            + tm * MIX_WIDTH * 2 + 6 * tm * d * 4)
    return pl.pallas_call(
        functools.partial(_outproj_kernel, mod_row=mod_row, alpha=alpha),
        out_shape=jax.ShapeDtypeStruct((n_rows, d), F32),
        grid=(n_rows // tm,),
        in_specs=[row(d, 0),
                  pl.BlockSpec((1, N_MOD, d), lambda i: (idx(i), 0, 0)),
                  row(GLA_V_W, 0), row(GLA_V_W, 0), row(GLA_V_W, OFF_GLA_R // GLA_V_W),
                  row(DN_V_W, 0), row(DN_V_W, 0), row(DN_V_W, OFF_DN_GATE // DN_V_W),
                  row(ATT_Q_W, 0),
                  vec(GLA_DV), vec(DN_DV),
                  pl.BlockSpec((MIX_WIDTH, d), lambda i: (0, 0)),
                  vec(d), vec(d)],
        out_specs=row(d, 0),
        compiler_params=_cparams(1, vmem),
        name="mixer_out_proj",
    )(x, mod, gla_f, gla_b, p, dn_f, dn_b, p, att, gla_norm_g, dn_norm_g, w_out, ln_g, ln_b)


def _rope_tables(seq):
    t = jnp.arange(seq)
    pr = (t // GRID_W).astype(F32)
    pc = (t % GRID_W).astype(F32)
    axis_dim = ATT_HD // 2
    inv = ROPE_THETA ** (-jnp.arange(0, axis_dim, 2, dtype=F32) / axis_dim)
    ar = pr[:, None] * inv
    ac = pc[:, None] * inv
    cos = jnp.concatenate([jnp.cos(ar), jnp.cos(ar), jnp.cos(ac), jnp.cos(ac)], axis=1)
    sin = jnp.concatenate([-jnp.sin(ar), jnp.sin(ar), -jnp.sin(ac), jnp.sin(ac)], axis=1)
    return cos, sin


def kernel(x, c, ctx, c_ctx, w_ada, b_ada, ln_g, ln_b, w_ffn_gate, w_ffn_up, w_ffn_down, w_in,
           gla_wa1, gla_wa2, gla_ba, gla_norm_g, dn_conv, dn_wab, dn_a_log, dn_dt_bias, dn_norm_g,
           q_norm_g, k_norm_g, w_out):
    n_batch, seq, d = x.shape
    ctx_len = ctx.shape[1]
    depth = w_ada.shape[0]
    alpha = (2 * depth) ** 0.25
    n_lat = n_batch * seq
    n_all = n_lat + n_batch * ctx_len
    assert n_batch + 1 <= 8 and seq % SEQ_BLOCK == 0 and ctx_len == SEQ_BLOCK

    cond = jnp.zeros((8, d), F32).at[:n_batch].set(c).at[n_batch].set(c_ctx)
    mod_all = _ada_table(cond, w_ada, b_ada).reshape(depth, 8, N_MOD, d)
    rope_cos, rope_sin = _rope_tables(seq)
    xs = jnp.concatenate([x.reshape(n_lat, d), ctx.reshape(n_batch * ctx_len, d)], axis=0)

    for layer in range(depth):
        last = layer == depth - 1
        mod = mod_all[layer, :n_batch + 1]
        lg = ln_g[layer].reshape(3, 1, d)
        lb = ln_b[layer].reshape(3, 1, d)
        ffn_w = [(w_ffn_gate[layer, i].astype(BF16), w_ffn_up[layer, i].astype(BF16),
                  w_ffn_down[layer, i].astype(BF16)) for i in range(2)]
        w_ex = jnp.concatenate([gla_wa1[layer, 0], gla_wa1[layer, 1], dn_wab[layer, 0], dn_wab[layer, 1]], axis=1)
        w_ex = jnp.pad(w_ex, ((0, 0), (0, EX_WIDTH - w_ex.shape[1]))).astype(BF16)
        common = dict(n_lat_rows=n_lat, seq=seq)

        xs = _ffn(xs, mod, *ffn_w[0], lg[0], lb[0], mod_row=0, n_rows=n_all, alpha=alpha, **common)
        p, ex = _inproj(xs, mod, w_in[layer].astype(BF16), w_ex, mod_row=3, **common)
        dn_x, aq, ak, avt = _prep(p, dn_conv[layer], q_norm_g[layer].reshape(1, ATT_HD),
                              k_norm_g[layer].reshape(1, ATT_HD), rope_cos, rope_sin, **common)
        gla_f, gla_b = _gla(p, ex, gla_wa2[layer], gla_ba[layer].reshape(2, 1, GLA_QK_W),
                            n_batch=n_batch, **common)
        dn_f, dn_b = _dn(dn_x, ex, dn_a_log[layer].reshape(2, 1, DN_HEADS),
                         dn_dt_bias[layer].reshape(2, 1, DN_HEADS), n_batch=n_batch, **common)
        att = _attention(aq, ak, avt, n_batch=n_batch, ctx_len=ctx_len, with_ctx=not last, **common)
        n_out = n_lat if last else n_all
        xs = _outproj(xs, mod, gla_f, gla_b, p, dn_f, dn_b, att, gla_norm_g[layer].reshape(1, GLA_DV),
                      dn_norm_g[layer].reshape(1, DN_DV), w_out[layer].astype(BF16), lg[1], lb[1],
                      mod_row=5, n_rows=n_out, alpha=alpha, **common)
        xs = _ffn(xs, mod, *ffn_w[1], lg[2], lb[2], mod_row=6, n_rows=n_out, alpha=alpha, **common)
    return xs.reshape(n_batch, seq, d)
```

```python
import functools
import math

import jax
import jax.numpy as jnp
from jax import lax
from jax.experimental import pallas as pl
from jax.experimental.pallas import tpu as pltpu

F32 = jnp.float32
BF16 = jnp.bfloat16

N_MOD = 9
GRID_W = 64
MACARON_WEIGHT = 0.5
GLA_HEADS, GLA_DK, GLA_DV, GLA_RANK, GLA_TAU = 4, 64, 128, 16, 16.0
DN_HEADS, DN_DK, DN_DV, DN_CONV = 4, 128, 128, 3
ATT_HEADS, ATT_KV_HEADS, ATT_HD = 8, 2, 128
ATT_GROUP = ATT_HEADS // ATT_KV_HEADS
VT_ROWS = ATT_HD + 16
ATT_LOOKAHEAD = 3
CHUNK = 64
ROPE_THETA = 10000.0
NORM_EPS = 1e-6
LOG2_E = math.log2(math.e)
GLA_QK_W, GLA_V_W = GLA_HEADS * GLA_DK, GLA_HEADS * GLA_DV
DN_QK_W, DN_V_W = DN_HEADS * DN_DK, DN_HEADS * DN_DV
ATT_Q_W, ATT_KV_W = ATT_HEADS * ATT_HD, ATT_KV_HEADS * ATT_HD
IN_WIDTH = 2 * GLA_QK_W + 2 * GLA_V_W + 2 * DN_QK_W + 2 * DN_V_W + ATT_Q_W + 2 * ATT_KV_W
MIX_WIDTH = GLA_V_W + DN_V_W + ATT_Q_W
OFF_GLA_V, OFF_GLA_R = 2 * GLA_QK_W, 2 * GLA_QK_W + GLA_V_W
OFF_DN = OFF_GLA_R + GLA_V_W
OFF_DN_GATE = OFF_DN + 2 * DN_QK_W + DN_V_W
OFF_ATT = OFF_DN_GATE + DN_V_W
EX_WIDTH = 128
EX_DN = 2 * GLA_RANK

V7X_VMEM_BYTES = 64 * 1024 * 1024
LANES = 128
SEQ_BLOCK = 256
CHUNKS_PER_BLOCK = SEQ_BLOCK // CHUNK


def _cparams(n_axes, vmem_bytes):
    limit = int(min(max(vmem_bytes, 16 * 1024 * 1024), V7X_VMEM_BYTES - 8 * 1024 * 1024))
    return pltpu.CompilerParams(dimension_semantics=("arbitrary",) * n_axes,
                                vmem_limit_bytes=limit)


def _dot(a, b):
    return jnp.dot(a, b, preferred_element_type=F32)


def _dot_nt(a, b):
    return lax.dot_general(a, b, (((1,), (1,)), ((), ())), preferred_element_type=F32)


def _dot_tn(a, b):
    return lax.dot_general(a, b, (((0,), (0,)), ((), ())), preferred_element_type=F32)


def _sigmoid(x):
    return 1.0 / (1.0 + jnp.exp(-x))


def _silu(x):
    return x * _sigmoid(x)


def _softplus(x):
    return jnp.maximum(x, 0.0) + jnp.log(1.0 + jnp.exp(-jnp.abs(x)))


def _layernorm(z, g, b):
    mu = jnp.mean(z, axis=-1, keepdims=True)
    zc = z - mu
    var = jnp.mean(zc * zc, axis=-1, keepdims=True)
    return zc * lax.rsqrt(var + NORM_EPS) * g + b


def _head_rmsnorm(x, g):
    return x * lax.rsqrt(jnp.mean(x * x, axis=-1, keepdims=True) + NORM_EPS) * g


def _cumsum_f32(tri, g):
    g1 = g.astype(BF16)
    r1 = g - g1.astype(F32)
    g2 = r1.astype(BF16)
    g3 = (r1 - g2.astype(F32)).astype(BF16)
    return _dot(tri, g1) + _dot(tri, g2) + _dot(tri, g3)


def _pick(n, candidates):
    for c in candidates:
        if n % c == 0:
            return c
    raise ValueError(f"no block size in {candidates} divides {n}")


def _mod_block_index(i, tm, n_lat_rows, seq, n_batch):
    assert seq % tm == 0 and n_lat_rows % tm == 0
    r = i * tm
    return jnp.where(r < n_lat_rows, r // seq, n_batch)


def _ada_kernel(c_ref, w_ref, b_ref, o_ref):
    s = _silu(c_ref[...]).astype(BF16)
    o_ref[0] = _dot(s, w_ref[0].astype(BF16)) + b_ref[0]


def _ada_table(cond, w_ada, b_ada):
    n_layer, d, n = w_ada.shape
    tn = _pick(n, (1024, 512, 256, 128))
    vmem = 2 * d * tn * 4 + d * tn * 2 + 4 * 8 * (d + 2 * tn) * 4
    return pl.pallas_call(
        _ada_kernel,
        out_shape=jax.ShapeDtypeStruct((n_layer, 8, n), F32),
        grid=(n_layer, n // tn),
        in_specs=[pl.BlockSpec((8, d), lambda l, j: (0, 0)),
                  pl.BlockSpec((1, d, tn), lambda l, j: (l, 0, j)),
                  pl.BlockSpec((1, 1, tn), lambda l, j: (l, 0, j))],
        out_specs=pl.BlockSpec((1, 8, tn), lambda l, j: (l, 0, j)),
        compiler_params=_cparams(2, vmem),
        name="ada_table",
    )(cond, w_ada, b_ada.reshape(n_layer, 1, n))


def _ffn_kernel(x_ref, mod_ref, wg_ref, wu_ref, wd_ref, g_ref, b_ref, o_ref, h_ref, acc_ref,
                *, mod_row, alpha):
    j = pl.program_id(1)

    @pl.when(j == 0)
    def _():
        shift = mod_ref[0, mod_row:mod_row + 1, :]
        scale = mod_ref[0, mod_row + 1:mod_row + 2, :]
        h_ref[...] = (x_ref[...] * (1.0 + scale) + shift).astype(BF16)
        acc_ref[...] = jnp.zeros(acc_ref.shape, F32)

    h = h_ref[...]
    a = _dot(h, wg_ref[...])
    u = _dot(h, wu_ref[...])
    acc_ref[...] += _dot((_silu(a) * u).astype(BF16), wd_ref[...])

    @pl.when(j == pl.num_programs(1) - 1)
    def _():
        gate = mod_ref[0, mod_row + 2:mod_row + 3, :]
        z = alpha * x_ref[...] + gate * (MACARON_WEIGHT * acc_ref[...])
        o_ref[...] = _layernorm(z, g_ref[...], b_ref[...])


def _ffn(x, mod, wg, wu, wd, ln_g, ln_b, *, mod_row, n_rows, n_lat_rows, seq, alpha):
    d = x.shape[1]
    f = wg.shape[1]
    n_batch = mod.shape[0] - 1
    tm = _pick(n_rows, (512, 256))
    tf = _pick(f, (512, 256, 128))
    vmem = (4 * tm * d * 4 + tm * d * 4 + tm * d * 2 + 2 * 3 * d * tf * 2
            + 8 * tm * tf * 4 + 2 * 16 * d * 4)
    idx = functools.partial(_mod_block_index, tm=tm, n_lat_rows=n_lat_rows, seq=seq, n_batch=n_batch)
    return pl.pallas_call(
        functools.partial(_ffn_kernel, mod_row=mod_row, alpha=alpha),
        out_shape=jax.ShapeDtypeStruct((n_rows, d), F32),
        grid=(n_rows // tm, f // tf),
        in_specs=[pl.BlockSpec((tm, d), lambda i, j: (i, 0)),
                  pl.BlockSpec((1, N_MOD, d), lambda i, j: (idx(i), 0, 0)),
                  pl.BlockSpec((d, tf), lambda i, j: (0, j)),
                  pl.BlockSpec((d, tf), lambda i, j: (0, j)),
                  pl.BlockSpec((tf, d), lambda i, j: (j, 0)),
                  pl.BlockSpec((1, d), lambda i, j: (0, 0)),
                  pl.BlockSpec((1, d), lambda i, j: (0, 0))],
        out_specs=pl.BlockSpec((tm, d), lambda i, j: (i, 0)),
        scratch_shapes=[pltpu.VMEM((tm, d), BF16), pltpu.VMEM((tm, d), F32)],
        compiler_params=_cparams(2, vmem),
        name="ffn_sublayer",
    )(x, mod, wg, wu, wd, ln_g, ln_b)


def _inproj_kernel(x_ref, mod_ref, w_ref, wex_ref, p_ref, ex_ref, h_ref, *, mod_row):
    j = pl.program_id(1)

    @pl.when(j == 0)
    def _():
        shift = mod_ref[0, mod_row:mod_row + 1, :]
        scale = mod_ref[0, mod_row + 1:mod_row + 2, :]
        h = (x_ref[...] * (1.0 + scale) + shift).astype(BF16)
        h_ref[...] = h
        ex_ref[...] = _dot(h, wex_ref[...])

    p_ref[...] = _dot(h_ref[...], w_ref[...])


def _inproj(x, mod, w_in, w_ex, *, mod_row, n_lat_rows, seq):
    n_rows, d = x.shape
    n = w_in.shape[1]
    n_batch = mod.shape[0] - 1
    tm = _pick(n_rows, (1024, 512, 256))
    tn = _pick(n, (512, 256, 128))
    vmem = (2 * tm * d * 4 + tm * d * 2 + 2 * d * tn * 2 + 2 * d * EX_WIDTH * 2
            + 4 * tm * tn * 4 + 2 * tm * EX_WIDTH * 4 + 2 * 16 * d * 4 + 2 * tm * d * 4)
    idx = functools.partial(_mod_block_index, tm=tm, n_lat_rows=n_lat_rows, seq=seq, n_batch=n_batch)
    return pl.pallas_call(
        functools.partial(_inproj_kernel, mod_row=mod_row),
        out_shape=(jax.ShapeDtypeStruct((n_rows, n), F32),
                   jax.ShapeDtypeStruct((n_rows, EX_WIDTH), F32)),
        grid=(n_rows // tm, n // tn),
        in_specs=[pl.BlockSpec((tm, d), lambda i, j: (i, 0)),
                  pl.BlockSpec((1, N_MOD, d), lambda i, j: (idx(i), 0, 0)),
                  pl.BlockSpec((d, tn), lambda i, j: (0, j)),
                  pl.BlockSpec((d, EX_WIDTH), lambda i, j: (0, 0))],
        out_specs=(pl.BlockSpec((tm, tn), lambda i, j: (i, j)),
                   pl.BlockSpec((tm, EX_WIDTH), lambda i, j: (i, 0))),
        scratch_shapes=[pltpu.VMEM((tm, d), BF16)],
        compiler_params=_cparams(2, vmem),
        name="mixer_in_proj",
    )(x, mod, w_in, w_ex)


def _prep_kernel(dn_ref, prev_ref, next_ref, conv_ref, aq0_ref, aq1_ref, akv_ref, qg_ref, kg_ref,
                 cos_ref, sin_ref, dn_o, aq_o, ak_o, avt_o, *, n_lat_blocks, blocks_per_seq):
    i = pl.program_id(0)
    is_lat = i < n_lat_blocks
    pos = i % blocks_per_seq
    has_prev = jnp.logical_and(is_lat, pos != 0)
    has_next = jnp.logical_and(is_lat, pos != blocks_per_seq - 1)

    z = dn_ref[...]
    rows = z.shape[0]
    ridx = lax.broadcasted_iota(jnp.int32, z.shape, 0)
    halo_p = jnp.where(has_prev, prev_ref[7:8, :], 0.0)
    halo_n = jnp.where(has_next, next_ref[0:1, :], 0.0)
    z_prev = jnp.where(ridx == 0, halo_p, pltpu.roll(z, 1, 0))
    z_next = jnp.where(ridx == rows - 1, halo_n, pltpu.roll(z, rows - 1, 0))
    y = _silu(z_prev * conv_ref[0:1, :] + z * conv_ref[1:2, :] + z_next * conv_ref[2:3, :])
    for hh in range(2 * DN_HEADS):
        sl = slice(hh * DN_DK, (hh + 1) * DN_DK)
        yh = y[:, sl]
        yn = yh * lax.rsqrt(jnp.sum(yh * yh, axis=-1, keepdims=True) + NORM_EPS)
        if hh < DN_HEADS:
            yn = yn * (DN_DK ** -0.5)
        dn_o[:, sl] = yn
    dn_o[:, 2 * DN_QK_W:] = y[:, 2 * DN_QK_W:]

    lane = lax.broadcasted_iota(jnp.int32, (rows, ATT_HD), 1)
    first_half_of_pair = (lane // (ATT_HD // 4)) % 2 == 0
    cos = cos_ref[...]
    sin = sin_ref[...]

    def rope(xh):
        swapped = jnp.where(first_half_of_pair,
                            pltpu.roll(xh, ATT_HD - ATT_HD // 4, 1), pltpu.roll(xh, ATT_HD // 4, 1))
        return jnp.where(is_lat, xh * cos + swapped * sin, xh)

    for hh in range(ATT_HEADS):
        src = aq0_ref if hh < ATT_HEADS // 2 else aq1_ref
        off = (hh % (ATT_HEADS // 2)) * ATT_HD
        qh = rope(_head_rmsnorm(src[:, off:off + ATT_HD], qg_ref[...])) * (LOG2_E * ATT_HD ** -0.5)
        aq_o[:, hh * ATT_HD:(hh + 1) * ATT_HD] = qh.astype(BF16)
    for hh in range(ATT_KV_HEADS):
        sl = slice(hh * ATT_HD, (hh + 1) * ATT_HD)
        ak_o[:, sl] = rope(_head_rmsnorm(akv_ref[:, sl], kg_ref[...])).astype(BF16)
        vt = akv_ref[:, ATT_KV_W + hh * ATT_HD:ATT_KV_W + (hh + 1) * ATT_HD].T.astype(BF16)
        avt_o[hh * VT_ROWS:hh * VT_ROWS + ATT_HD, :] = vt
        avt_o[hh * VT_ROWS + ATT_HD:(hh + 1) * VT_ROWS, :] = jnp.ones((VT_ROWS - ATT_HD, rows), BF16)


def _prep(p, conv_w, q_norm_g, k_norm_g, rope_cos, rope_sin, *, n_lat_rows, seq):
    n_rows = p.shape[0]
    tm = SEQ_BLOCK
    n_blocks = n_rows // tm
    n_lat_blocks = n_lat_rows // tm
    bps = seq // tm
    dn_w = 2 * DN_QK_W + DN_V_W
    dn_cb = OFF_DN // dn_w
    assert OFF_DN % dn_w == 0 and OFF_ATT % 512 == 0
    att_cb = OFF_ATT // 512
    halo_per_block = tm // 8
    n_halo = n_rows // 8
    vmem = (2 * 2 * tm * dn_w * 4 + 2 * 3 * tm * 512 * 4 + 2 * tm * 1536 * 2
            + 10 * tm * dn_w * 4 + 4 * tm * LANES * 4)
    return pl.pallas_call(
        functools.partial(_prep_kernel, n_lat_blocks=n_lat_blocks, blocks_per_seq=bps),
        out_shape=(jax.ShapeDtypeStruct((n_rows, dn_w), F32),
                   jax.ShapeDtypeStruct((n_rows, ATT_Q_W), BF16),
                   jax.ShapeDtypeStruct((n_rows, ATT_KV_W), BF16),
                   jax.ShapeDtypeStruct((ATT_KV_HEADS * VT_ROWS, n_rows), BF16)),
        grid=(n_blocks,),
        in_specs=[pl.BlockSpec((tm, dn_w), lambda i: (i, dn_cb)),
                  pl.BlockSpec((8, dn_w), lambda i: (jnp.maximum(i * halo_per_block - 1, 0), dn_cb)),
                  pl.BlockSpec((8, dn_w), lambda i: (jnp.minimum((i + 1) * halo_per_block, n_halo - 1), dn_cb)),
                  pl.BlockSpec((DN_CONV, dn_w), lambda i: (0, 0)),
                  pl.BlockSpec((tm, 512), lambda i: (i, att_cb)),
                  pl.BlockSpec((tm, 512), lambda i: (i, att_cb + 1)),
                  pl.BlockSpec((tm, 512), lambda i: (i, att_cb + 2)),
                  pl.BlockSpec((1, ATT_HD), lambda i: (0, 0)),
                  pl.BlockSpec((1, ATT_HD), lambda i: (0, 0)),
                  pl.BlockSpec((tm, ATT_HD), lambda i: (jnp.where(i < n_lat_blocks, i % bps, 0), 0)),
                  pl.BlockSpec((tm, ATT_HD), lambda i: (jnp.where(i < n_lat_blocks, i % bps, 0), 0))],
        out_specs=(pl.BlockSpec((tm, dn_w), lambda i: (i, 0)),
                   pl.BlockSpec((tm, ATT_Q_W), lambda i: (i, 0)),
                   pl.BlockSpec((tm, ATT_KV_W), lambda i: (i, 0)),
                   pl.BlockSpec((ATT_KV_HEADS * VT_ROWS, tm), lambda i: (0, i))),
        compiler_params=_cparams(1, vmem),
        name="mixer_prep",
    )(p, p, p, conv_w, p, p, p, q_norm_g, k_norm_g, rope_cos, rope_sin)


def _seq_block(b, s, *, rev, n_lat_blocks, blocks_per_seq):
    j = (blocks_per_seq - s) if rev else (s - 1)
    return jnp.where(s == 0, n_lat_blocks + b, b * blocks_per_seq + j)


def _tri_masks(rev):
    row = lax.broadcasted_iota(jnp.int32, (CHUNK, CHUNK), 0)
    col = lax.broadcasted_iota(jnp.int32, (CHUNK, CHUNK), 1)
    incl = (row <= col) if rev else (row >= col)
    strict = (row < col) if rev else (row > col)
    return incl, strict


def _gla_kernel(*refs, n_batch):
    n_in = n_batch * 2 * 4
    wa2_ref, ba_ref, of_ref, ob_ref, st_ref = refs[n_in:]
    o_refs = (of_ref, ob_ref)
    streams = [(b, d) + tuple(refs[(b * 2 + d) * 4:(b * 2 + d + 1) * 4])
               for b in range(n_batch) for d in range(2)]

    @pl.when(pl.program_id(0) == 0)
    def _():
        st_ref[...] = jnp.zeros(st_ref.shape, F32)

    masks = [_tri_masks(rev)[0] for rev in (False, True)]
    tris = [jnp.where(m, 1.0, 0.0).astype(BF16) for m in masks]
    units = [(b, d, h) for b in range(n_batch) for d in range(2) for h in range(GLA_HEADS)]
    hs = lambda h: slice(h * GLA_DK, (h + 1) * GLA_DK)
    vs = lambda h: slice(h * GLA_DV, (h + 1) * GLA_DV)

    def chunk_step(c, carry):
        prep = {}
        for b, d, q_ref, k_ref, v_ref, e_ref in streams:
            rev = d == 1
            cc = (CHUNKS_PER_BLOCK - 1 - c) if rev else c
            rows = pl.ds(pl.multiple_of(cc * CHUNK, CHUNK), CHUNK)
            low_rank = e_ref[rows, d * GLA_RANK:(d + 1) * GLA_RANK].astype(BF16)
            logit = _dot(low_rank, wa2_ref[d].astype(BF16)) + ba_ref[d]
            log_a = -_softplus(-logit) / GLA_TAU
            bcum = _cumsum_f32(tris[d], log_a)
            blast = bcum[0:1] if rev else bcum[CHUNK - 1:CHUNK]
            kc = k_ref[rows, :]
            prep[b, d] = dict(
                rows=rows,
                q_in=(q_ref[rows, :] * (GLA_DK ** -0.5) * jnp.exp(bcum)).astype(BF16),
                k_in=(kc * jnp.exp(-bcum)).astype(BF16),
                k_end=(kc * jnp.exp(blast - bcum)).astype(BF16),
                decay=jnp.exp(blast),
                v=v_ref[rows, :].astype(BF16),
                st=st_ref[b, d])
        attn = {(b, d, h): jnp.where(masks[d], _dot_nt(prep[b, d]["q_in"][:, hs(h)],
                                                        prep[b, d]["k_in"][:, hs(h)]), 0.0).astype(BF16)
                for b, d, h in units}
        inter = {(b, d, h): _dot_nt(prep[b, d]["q_in"][:, hs(h)], prep[b, d]["st"][:, hs(h)].astype(BF16))
                 for b, d, h in units}
        intra = {(b, d, h): _dot(attn[b, d, h], prep[b, d]["v"][:, vs(h)]) for b, d, h in units}
        upd = {(b, d, h): _dot_tn(prep[b, d]["v"][:, vs(h)], prep[b, d]["k_end"][:, hs(h)])
               for b, d, h in units}
        for (b, d), pr in prep.items():
            o_refs[d][b, pr["rows"], :] = jnp.concatenate(
                [intra[b, d, h] + inter[b, d, h] for h in range(GLA_HEADS)], axis=1)
            st_ref[b, d] = pr["st"] * pr["decay"] + jnp.concatenate(
                [upd[b, d, h] for h in range(GLA_HEADS)], axis=1)
        return carry

    lax.fori_loop(0, CHUNKS_PER_BLOCK, chunk_step, 0)


def _seq_out_specs(n_batch, tm, width, bps):
    return (pl.BlockSpec((n_batch, tm, width), lambda s: (0, s, 0)),
            pl.BlockSpec((n_batch, tm, width), lambda s: (0, jnp.where(s == 0, 0, bps + 1 - s), 0)))


def _gla(p, ex, wa2, ba, *, n_batch, n_lat_rows, seq):
    tm = SEQ_BLOCK
    bps = seq // tm
    nlb = n_lat_rows // tm
    fwd = functools.partial(_seq_block, rev=False, n_lat_blocks=nlb, blocks_per_seq=bps)
    bwd = functools.partial(_seq_block, rev=True, n_lat_blocks=nlb, blocks_per_seq=bps)
    v_cb = OFF_GLA_V // GLA_V_W

    def specs(b, blk):
        return [pl.BlockSpec((tm, GLA_QK_W), lambda s: (blk(b, s), 0)),
                pl.BlockSpec((tm, GLA_QK_W), lambda s: (blk(b, s), 1)),
                pl.BlockSpec((tm, GLA_V_W), lambda s: (blk(b, s), v_cb)),
                pl.BlockSpec((tm, EX_WIDTH), lambda s: (blk(b, s), 0))]

    n_streams = 2 * n_batch
    vmem = (2 * n_streams * tm * (2 * GLA_QK_W + GLA_V_W + EX_WIDTH) * 4 + 2 * n_streams * tm * GLA_V_W * 4
            + n_streams * GLA_DV * GLA_QK_W * 4 + n_streams * 40 * CHUNK * GLA_QK_W * 4)
    out = jax.ShapeDtypeStruct((n_batch, seq + tm, GLA_V_W), F32)
    return pl.pallas_call(
        functools.partial(_gla_kernel, n_batch=n_batch),
        out_shape=(out, out),
        grid=(bps + 1,),
        in_specs=[spec for b in range(n_batch) for blk in (fwd, bwd) for spec in specs(b, blk)] + [
            pl.BlockSpec((2, GLA_RANK, GLA_QK_W), lambda s: (0, 0, 0)),
            pl.BlockSpec((2, 1, GLA_QK_W), lambda s: (0, 0, 0))],
        out_specs=_seq_out_specs(n_batch, tm, GLA_V_W, bps),
        scratch_shapes=[pltpu.VMEM((n_batch, 2, GLA_DV, GLA_QK_W), F32)],
        compiler_params=_cparams(1, vmem),
        name="gla_scan",
    )(*([p, p, p, ex] * n_streams), wa2, ba)


def _dn_kernel(*refs, n_batch):
    n_in = n_batch * 2 * 2
    alog_ref, dtb_ref, of_ref, ob_ref, st_ref = refs[n_in:]
    o_refs = (of_ref, ob_ref)
    streams = [(b, d) + tuple(refs[(b * 2 + d) * 2:(b * 2 + d + 1) * 2])
               for b in range(n_batch) for d in range(2)]

    @pl.when(pl.program_id(0) == 0)
    def _():
        st_ref[...] = jnp.zeros(st_ref.shape, F32)

    masks = [_tri_masks(rev) for rev in (False, True)]
    tris = [jnp.where(m[0], 1.0, 0.0).astype(BF16) for m in masks]
    row = lax.broadcasted_iota(jnp.int32, (CHUNK, CHUNK), 0)
    col = lax.broadcasted_iota(jnp.int32, (CHUNK, CHUNK), 1)
    eye = jnp.where(row == col, 1.0, 0.0)

    def chunk_step(c, carry):
        units = []
        for b, d, x_ref, e_ref in streams:
            rev = d == 1
            cc = (CHUNKS_PER_BLOCK - 1 - c) if rev else c
            rows = pl.ds(pl.multiple_of(cc * CHUNK, CHUNK), CHUNK)
            lo = EX_DN + 2 * DN_HEADS * d
            a_in = e_ref[rows, lo:lo + DN_HEADS]
            b_in = e_ref[rows, lo + DN_HEADS:lo + 2 * DN_HEADS]
            g = -jnp.exp(alog_ref[d]) * _softplus(a_in + dtb_ref[d])
            beta_all = _sigmoid(b_in)
            gcum_all = _cumsum_f32(tris[d], g)
            for h in range(DN_HEADS):
                gcum = gcum_all[:, h:h + 1]
                glast = gcum[0:1] if rev else gcum[CHUNK - 1:CHUNK]
                gmat = jnp.broadcast_to(gcum, (CHUNK, CHUNK))
                units.append(dict(
                    b=b, d=d, h=h, rows=rows, beta=beta_all[:, h:h + 1], gcum=gcum, glast=glast,
                    decay=jnp.exp(gmat - gmat.T),
                    q=x_ref[rows, h * DN_DK:(h + 1) * DN_DK],
                    k=x_ref[rows, DN_QK_W + h * DN_DK:DN_QK_W + (h + 1) * DN_DK],
                    v=x_ref[rows, 2 * DN_QK_W + h * DN_DV:2 * DN_QK_W + (h + 1) * DN_DV]))
        for u in units:
            u["kb"] = u["k"] * u["beta"]
            u["qk"] = _dot_nt(jnp.concatenate([u["q"], u["kb"]], axis=0).astype(BF16), u["k"].astype(BF16))
        for u in units:
            incl, strict = masks[u["d"]]
            u["attn"] = jnp.where(incl, u["qk"][:CHUNK] * u["decay"], 0.0).astype(BF16)
            n = -jnp.where(strict, u["qk"][CHUNK:] * u["decay"], 0.0)
            u["t"] = eye + n
            u["n"] = n.astype(BF16)
        for u in units:
            u["p"] = _dot(u["n"], u["n"])
        for _ in range(int(math.log2(CHUNK)) - 2):
            for u in units:
                u["both"] = _dot(jnp.concatenate([u["t"], u["p"]], axis=0).astype(BF16), u["p"].astype(BF16))
            for u in units:
                u["t"] = u["t"] + u["both"][:CHUNK]
                u["p"] = u["both"][CHUNK:]
        for u in units:
            u["tp"] = _dot(u["t"].astype(BF16), u["p"].astype(BF16))
        for u in units:
            tinv = (u["t"] + u["tp"]).astype(BF16)
            egc = jnp.exp(u["gcum"])
            rhs = jnp.concatenate([u["v"] * u["beta"], u["kb"] * egc], axis=1).astype(BF16)
            u["uw"] = _dot(tinv, rhs)
            u["q_dec"] = u["q"] * egc
            u["k_end"] = (u["k"] * jnp.exp(u["glast"] - u["gcum"])).astype(BF16)
        for u in units:
            u["state"] = st_ref[u["b"], u["d"], u["h"]]
            lhs = jnp.concatenate([u["uw"][:, DN_DV:], u["q_dec"]], axis=0).astype(BF16)
            u["wq"] = _dot(lhs, u["state"].astype(BF16))
        for u in units:
            u["v_new"] = (u["uw"][:, :DN_DV] - u["wq"][:CHUNK]).astype(BF16)
            u["o"] = u["wq"][CHUNK:] + _dot(u["attn"], u["v_new"])
        for u in units:
            st_ref[u["b"], u["d"], u["h"]] = (u["state"] * jnp.exp(u["glast"])
                                              + _dot_tn(u["k_end"], u["v_new"]))
        for b, d, _, _ in streams:
            sel = [u for u in units if u["b"] == b and u["d"] == d]
            o_refs[d][b, sel[0]["rows"], :] = jnp.concatenate([u["o"] for u in sel], axis=1)
        return carry

    lax.fori_loop(0, CHUNKS_PER_BLOCK, chunk_step, 0)


def _dn(x, ex, a_log, dt_bias, *, n_batch, n_lat_rows, seq):
    w = x.shape[1]
    tm = SEQ_BLOCK
    bps = seq // tm
    nlb = n_lat_rows // tm
    fwd = functools.partial(_seq_block, rev=False, n_lat_blocks=nlb, blocks_per_seq=bps)
    bwd = functools.partial(_seq_block, rev=True, n_lat_blocks=nlb, blocks_per_seq=bps)

    def specs(b, blk):
        return [pl.BlockSpec((tm, w), lambda s: (blk(b, s), 0)),
                pl.BlockSpec((tm, EX_WIDTH), lambda s: (blk(b, s), 0))]

    n_streams = 2 * n_batch
    vmem = (2 * n_streams * tm * (w + EX_WIDTH) * 4 + 2 * n_streams * tm * DN_V_W * 4
            + n_streams * DN_HEADS * DN_DK * DN_DV * 4 + n_streams * DN_HEADS * 40 * CHUNK * DN_DK * 4)
    out = jax.ShapeDtypeStruct((n_batch, seq + tm, DN_V_W), F32)
    return pl.pallas_call(
        functools.partial(_dn_kernel, n_batch=n_batch),
        out_shape=(out, out),
        grid=(bps + 1,),
        in_specs=[spec for b in range(n_batch) for blk in (fwd, bwd) for spec in specs(b, blk)] + [
            pl.BlockSpec((2, 1, DN_HEADS), lambda s: (0, 0, 0)),
            pl.BlockSpec((2, 1, DN_HEADS), lambda s: (0, 0, 0))],
        out_specs=_seq_out_specs(n_batch, tm, DN_V_W, bps),
        scratch_shapes=[pltpu.VMEM((n_batch, 2, DN_HEADS, DN_DK, DN_DV), F32)],
        compiler_params=_cparams(1, vmem),
        name="deltanet_scan",
    )(*([x, ex] * n_streams), a_log, dt_bias)


def _attn_kernel(q_ref, kc_ref, vtc_ref, kl_ref, vtl_ref, o_ref, m_ref, acc_ref,
                 *, with_ctx, tk):
    tq = q_ref.shape[0]
    ctx_chunk = (lambda: kc_ref[...], lambda: vtc_ref[...])
    lat_chunks = [(lambda c=c: kl_ref[c * tk:(c + 1) * tk, :], lambda c=c: vtl_ref[:, c * tk:(c + 1) * tk])
                  for c in range(kl_ref.shape[0] // tk)]

    def scores(chunk, g):
        return _dot_nt(chunk[0](), q_ref[:, g * ATT_HD:(g + 1) * ATT_HD])

    def accumulate(chunk, g, s, first):
        cols = slice(g * tq, (g + 1) * tq)
        m_new = jnp.max(s, axis=0, keepdims=True)
        if not first:
            m_old = m_ref[:, cols]
            m_new = jnp.maximum(m_old, m_new)
            alpha = jnp.exp2(m_old - m_new)
        p = jnp.exp2(s - m_new).astype(BF16)
        pv = _dot(chunk[1](), p)
        if not first:
            pv = alpha * acc_ref[:, cols] + pv
        m_ref[:, cols] = m_new
        acc_ref[:, cols] = pv

    def attend(chunks):
        units = [(ci, g) for ci in range(len(chunks)) for g in range(ATT_GROUP)]
        pending = [scores(chunks[ci], g) for ci, g in units[:ATT_LOOKAHEAD]]
        for i, (ci, g) in enumerate(units):
            if i + ATT_LOOKAHEAD < len(units):
                nci, ng = units[i + ATT_LOOKAHEAD]
                pending.append(scores(chunks[nci], ng))
            accumulate(chunks[ci], g, pending.pop(0), first=ci == 0)
        for g in range(ATT_GROUP):
            cols = slice(g * tq, (g + 1) * tq)
            o = acc_ref[:ATT_HD, cols] / acc_ref[ATT_HD:ATT_HD + 1, cols]
            o_ref[:, g * ATT_HD:(g + 1) * ATT_HD] = o.T

    def latent_query():
        attend([ctx_chunk] + lat_chunks)

    def context_query():
        attend([ctx_chunk])

    if with_ctx:
        is_ctx = pl.program_id(2) == 0
        pl.when(is_ctx)(context_query)
        pl.when(jnp.logical_not(is_ctx))(latent_query)
    else:
        latent_query()


def _attention(aq, ak, avt, *, n_batch, n_lat_rows, seq, ctx_len, with_ctx):
    n_rows = aq.shape[0]
    tq = SEQ_BLOCK
    assert ctx_len == tq
    bps = seq // tq
    nlb = n_lat_rows // tq
    tk = _pick(seq, (512, 256))
    n_q = bps + (1 if with_ctx else 0)
    q_cw = ATT_GROUP * ATT_HD

    def q_blk(b, qi):
        if with_ctx:
            return jnp.where(qi == 0, nlb + b, b * bps + qi - 1)
        return b * bps + qi

    vmem = (2 * tq * q_cw * (2 + 4) + 2 * 2 * (seq + ctx_len) * ATT_HD * 2
            + ATT_GROUP * tq * (2 * 8 + ATT_HD) * 4 + 6 * ATT_GROUP * tq * tk * 4)
    return pl.pallas_call(
        functools.partial(_attn_kernel, with_ctx=with_ctx, tk=tk),
        out_shape=jax.ShapeDtypeStruct((n_rows if with_ctx else n_lat_rows, ATT_Q_W), F32),
        grid=(n_batch, ATT_KV_HEADS, n_q),
        in_specs=[pl.BlockSpec((tq, q_cw), lambda b, h, qi: (q_blk(b, qi), h)),
                  pl.BlockSpec((ctx_len, ATT_HD), lambda b, h, qi: (nlb + b, h)),
                  pl.BlockSpec((VT_ROWS, ctx_len), lambda b, h, qi: (h, nlb + b)),
                  pl.BlockSpec((seq, ATT_HD), lambda b, h, qi: (b, h)),
                  pl.BlockSpec((VT_ROWS, seq), lambda b, h, qi: (h, b))],
        out_specs=pl.BlockSpec((tq, q_cw), lambda b, h, qi: (q_blk(b, qi), h)),
        scratch_shapes=[pltpu.VMEM((1, ATT_GROUP * tq), F32),
                        pltpu.VMEM((VT_ROWS, ATT_GROUP * tq), F32)],
        compiler_params=_cparams(3, vmem),
        name="gqa_attention",
    )(aq, ak, avt, ak, avt)


def _outproj_kernel(x_ref, mod_ref, gf_ref, gb_ref, gr_ref, df_ref, db_ref, dg_ref, at_ref,
                    gn_ref, dn_ref, w_ref, g_ref, b_ref, o_ref, *, mod_row, alpha):
    parts = []
    for f_ref, b_ref_, gate_ref, norm_ref, heads, dv in ((gf_ref, gb_ref, gr_ref, gn_ref, GLA_HEADS, GLA_DV),
                                                         (df_ref, db_ref, dg_ref, dn_ref, DN_HEADS, DN_DV)):
        for h in range(heads):
            sl = slice(h * dv, (h + 1) * dv)
            o = _head_rmsnorm(f_ref[0, :, sl] + b_ref_[0, :, sl], norm_ref[...]) * _silu(gate_ref[:, sl])
            parts.append(o.astype(BF16))
    parts.append(at_ref[...].astype(BF16))
    y = _dot(jnp.concatenate(parts, axis=1), w_ref[...])
    gate = mod_ref[0, mod_row:mod_row + 1, :]
    o_ref[...] = _layernorm(alpha * x_ref[...] + gate * y, g_ref[...], b_ref[...])


def _outproj(x, mod, gla_f, gla_b, p, dn_f, dn_b, att, gla_norm_g, dn_norm_g, w_out, ln_g, ln_b,
             *, mod_row, n_rows, n_lat_rows, seq, alpha):
    d = x.shape[1]
    n_batch = mod.shape[0] - 1
    tm = SEQ_BLOCK
    idx = functools.partial(_mod_block_index, tm=tm, n_lat_rows=n_lat_rows, seq=seq, n_batch=n_batch)
    row = lambda w, cb: pl.BlockSpec((tm, w), lambda i: (i, cb))
    vec = lambda w: pl.BlockSpec((1, w), lambda i: (0, 0))
    bps = seq // tm
    nlb = n_lat_rows // tm
    scan = lambda w: pl.BlockSpec((1, tm, w), lambda i: (jnp.where(i < nlb, i // bps, i - nlb),
                                                         jnp.where(i < nlb, 1 + i % bps, 0), 0))
    vmem = (2 * tm * (2 * d + 6 * GLA_V_W + ATT_Q_W) * 4 + 2 * MIX_WIDTH * d * 2
            + tm * MIX_WIDTH * 2 + 6 * tm * d * 4)
    return pl.pallas_call(
        functools.partial(_outproj_kernel, mod_row=mod_row, alpha=alpha),
        out_shape=jax.ShapeDtypeStruct((n_rows, d), F32),
        grid=(n_rows // tm,),
        in_specs=[row(d, 0),
                  pl.BlockSpec((1, N_MOD, d), lambda i: (idx(i), 0, 0)),
                  scan(GLA_V_W), scan(GLA_V_W), row(GLA_V_W, OFF_GLA_R // GLA_V_W),
                  scan(DN_V_W), scan(DN_V_W), row(DN_V_W, OFF_DN_GATE // DN_V_W),
                  row(ATT_Q_W, 0),
                  vec(GLA_DV), vec(DN_DV),
                  pl.BlockSpec((MIX_WIDTH, d), lambda i: (0, 0)),
                  vec(d), vec(d)],
        out_specs=row(d, 0),
        compiler_params=_cparams(1, vmem),
        name="mixer_out_proj",
    )(x, mod, gla_f, gla_b, p, dn_f, dn_b, p, att, gla_norm_g, dn_norm_g, w_out, ln_g, ln_b)


def _rope_tables(seq):
    t = jnp.arange(seq)
    pr = (t // GRID_W).astype(F32)
    pc = (t % GRID_W).astype(F32)
    axis_dim = ATT_HD // 2
    inv = ROPE_THETA ** (-jnp.arange(0, axis_dim, 2, dtype=F32) / axis_dim)
    ar = pr[:, None] * inv
    ac = pc[:, None] * inv
    cos = jnp.concatenate([jnp.cos(ar), jnp.cos(ar), jnp.cos(ac), jnp.cos(ac)], axis=1)
    sin = jnp.concatenate([-jnp.sin(ar), jnp.sin(ar), -jnp.sin(ac), jnp.sin(ac)], axis=1)
    return cos, sin


def kernel(x, c, ctx, c_ctx, w_ada, b_ada, ln_g, ln_b, w_ffn_gate, w_ffn_up, w_ffn_down, w_in,
           gla_wa1, gla_wa2, gla_ba, gla_norm_g, dn_conv, dn_wab, dn_a_log, dn_dt_bias, dn_norm_g,
           q_norm_g, k_norm_g, w_out):
    n_batch, seq, d = x.shape
    ctx_len = ctx.shape[1]
    depth = w_ada.shape[0]
    alpha = (2 * depth) ** 0.25
    n_lat = n_batch * seq
    n_all = n_lat + n_batch * ctx_len
    assert n_batch + 1 <= 8 and seq % SEQ_BLOCK == 0 and ctx_len == SEQ_BLOCK

    cond = jnp.zeros((8, d), F32).at[:n_batch].set(c).at[n_batch].set(c_ctx)
    mod_all = _ada_table(cond, w_ada, b_ada).reshape(depth, 8, N_MOD, d)
    rope_cos, rope_sin = _rope_tables(seq)
    xs = jnp.concatenate([x.reshape(n_lat, d), ctx.reshape(n_batch * ctx_len, d)], axis=0)

    for layer in range(depth):
        last = layer == depth - 1
        mod = mod_all[layer, :n_batch + 1]
        lg = ln_g[layer].reshape(3, 1, d)
        lb = ln_b[layer].reshape(3, 1, d)
        ffn_w = [(w_ffn_gate[layer, i].astype(BF16), w_ffn_up[layer, i].astype(BF16),
                  w_ffn_down[layer, i].astype(BF16)) for i in range(2)]
        w_ex = jnp.concatenate([gla_wa1[layer, 0], gla_wa1[layer, 1], dn_wab[layer, 0], dn_wab[layer, 1]], axis=1)
        w_ex = jnp.pad(w_ex, ((0, 0), (0, EX_WIDTH - w_ex.shape[1]))).astype(BF16)
        common = dict(n_lat_rows=n_lat, seq=seq)

        xs = _ffn(xs, mod, *ffn_w[0], lg[0], lb[0], mod_row=0, n_rows=n_all, alpha=alpha, **common)
        p, ex = _inproj(xs, mod, w_in[layer].astype(BF16), w_ex, mod_row=3, **common)
        dn_x, aq, ak, avt = _prep(p, dn_conv[layer], q_norm_g[layer].reshape(1, ATT_HD),
                              k_norm_g[layer].reshape(1, ATT_HD), rope_cos, rope_sin, **common)
        gla_f, gla_b = _gla(p, ex, gla_wa2[layer], gla_ba[layer].reshape(2, 1, GLA_QK_W),
                            n_batch=n_batch, **common)
        dn_f, dn_b = _dn(dn_x, ex, dn_a_log[layer].reshape(2, 1, DN_HEADS),
                         dn_dt_bias[layer].reshape(2, 1, DN_HEADS), n_batch=n_batch, **common)
        att = _attention(aq, ak, avt, n_batch=n_batch, ctx_len=ctx_len, with_ctx=not last, **common)
        n_out = n_lat if last else n_all
        xs = _outproj(xs, mod, gla_f, gla_b, p, dn_f, dn_b, att, gla_norm_g[layer].reshape(1, GLA_DV),
                      dn_norm_g[layer].reshape(1, DN_DV), w_out[layer].astype(BF16), lg[1], lb[1],
                      mod_row=5, n_rows=n_out, alpha=alpha, **common)
        xs = _ffn(xs, mod, *ffn_w[1], lg[2], lb[2], mod_row=6, n_rows=n_out, alpha=alpha, **common)
    return xs.reshape(n_batch, seq, d)
```

```python
import functools
import math

import jax
import jax.numpy as jnp
from jax import lax
from jax.experimental import pallas as pl
from jax.experimental.pallas import tpu as pltpu

F32 = jnp.float32
BF16 = jnp.bfloat16

N_MOD = 9
GRID_W = 64
MACARON_WEIGHT = 0.5
GLA_HEADS, GLA_DK, GLA_DV, GLA_RANK, GLA_TAU = 4, 64, 128, 16, 16.0
DN_HEADS, DN_DK, DN_DV, DN_CONV = 4, 128, 128, 3
ATT_HEADS, ATT_KV_HEADS, ATT_HD = 8, 2, 128
ATT_GROUP = ATT_HEADS // ATT_KV_HEADS
VT_ROWS = ATT_HD + 16
ATT_LOOKAHEAD = 3
CHUNK = 64
ROPE_THETA = 10000.0
NORM_EPS = 1e-6
LOG2_E = math.log2(math.e)
GLA_QK_W, GLA_V_W = GLA_HEADS * GLA_DK, GLA_HEADS * GLA_DV
DN_QK_W, DN_V_W = DN_HEADS * DN_DK, DN_HEADS * DN_DV
ATT_Q_W, ATT_KV_W = ATT_HEADS * ATT_HD, ATT_KV_HEADS * ATT_HD
IN_WIDTH = 2 * GLA_QK_W + 2 * GLA_V_W + 2 * DN_QK_W + 2 * DN_V_W + ATT_Q_W + 2 * ATT_KV_W
MIX_WIDTH = GLA_V_W + DN_V_W + ATT_Q_W
OFF_GLA_V, OFF_GLA_R = 2 * GLA_QK_W, 2 * GLA_QK_W + GLA_V_W
OFF_DN = OFF_GLA_R + GLA_V_W
OFF_DN_GATE = OFF_DN + 2 * DN_QK_W + DN_V_W
OFF_ATT = OFF_DN_GATE + DN_V_W
EX_WIDTH = 128
EX_DN = 2 * GLA_RANK

V7X_VMEM_BYTES = 64 * 1024 * 1024
LANES = 128
SEQ_BLOCK = 256
CHUNKS_PER_BLOCK = SEQ_BLOCK // CHUNK


def _cparams(n_axes, vmem_bytes):
    limit = int(min(max(vmem_bytes, 16 * 1024 * 1024), V7X_VMEM_BYTES - 8 * 1024 * 1024))
    return pltpu.CompilerParams(dimension_semantics=("arbitrary",) * n_axes,
                                vmem_limit_bytes=limit)


def _dot(a, b):
    return jnp.dot(a, b, preferred_element_type=F32)


def _dot_nt(a, b):
    return lax.dot_general(a, b, (((1,), (1,)), ((), ())), preferred_element_type=F32)


def _dot_tn(a, b):
    return lax.dot_general(a, b, (((0,), (0,)), ((), ())), preferred_element_type=F32)


def _sigmoid(x):
    return 1.0 / (1.0 + jnp.exp(-x))


def _silu(x):
    return x * _sigmoid(x)


def _softplus(x):
    return jnp.maximum(x, 0.0) + jnp.log(1.0 + jnp.exp(-jnp.abs(x)))


def _layernorm(z, g, b):
    mu = jnp.mean(z, axis=-1, keepdims=True)
    zc = z - mu
    var = jnp.mean(zc * zc, axis=-1, keepdims=True)
    return zc * lax.rsqrt(var + NORM_EPS) * g + b


def _head_rmsnorm(x, g):
    return x * lax.rsqrt(jnp.mean(x * x, axis=-1, keepdims=True) + NORM_EPS) * g


def _cumsum_f32(tri, g):
    g1 = g.astype(BF16)
    r1 = g - g1.astype(F32)
    g2 = r1.astype(BF16)
    g3 = (r1 - g2.astype(F32)).astype(BF16)
    return _dot(tri, g1) + _dot(tri, g2) + _dot(tri, g3)


def _pick(n, candidates):
    for c in candidates:
        if n % c == 0:
            return c
    raise ValueError(f"no block size in {candidates} divides {n}")


def _mod_block_index(i, tm, n_lat_rows, seq, n_batch):
    assert seq % tm == 0 and n_lat_rows % tm == 0
    r = i * tm
    return jnp.where(r < n_lat_rows, r // seq, n_batch)


def _ada_kernel(c_ref, w_ref, b_ref, o_ref):
    s = _silu(c_ref[...]).astype(BF16)
    o_ref[0] = _dot(s, w_ref[0].astype(BF16)) + b_ref[0]


def _ada_table(cond, w_ada, b_ada):
    n_layer, d, n = w_ada.shape
    tn = _pick(n, (1024, 512, 256, 128))
    vmem = 2 * d * tn * 4 + d * tn * 2 + 4 * 8 * (d + 2 * tn) * 4
    return pl.pallas_call(
        _ada_kernel,
        out_shape=jax.ShapeDtypeStruct((n_layer, 8, n), F32),
        grid=(n_layer, n // tn),
        in_specs=[pl.BlockSpec((8, d), lambda l, j: (0, 0)),
                  pl.BlockSpec((1, d, tn), lambda l, j: (l, 0, j)),
                  pl.BlockSpec((1, 1, tn), lambda l, j: (l, 0, j))],
        out_specs=pl.BlockSpec((1, 8, tn), lambda l, j: (l, 0, j)),
        compiler_params=_cparams(2, vmem),
        name="ada_table",
    )(cond, w_ada, b_ada.reshape(n_layer, 1, n))


def _ffn_kernel(*refs, mod_row, alpha, n_lat_blocks, split_x):
    if split_x:
        x_ref, xc_ref, *refs = refs
    else:
        x_ref, *refs = refs
    mod_ref, wg_ref, wu_ref, wd_ref, g_ref, b_ref, o_ref, h_ref, acc_ref, y_ref = refs
    j = pl.program_id(1)
    n_tiles = pl.num_programs(1) - 1

    def load_x():
        if not split_x:
            return x_ref[...]
        return jnp.where(pl.program_id(0) < n_lat_blocks, x_ref[...], xc_ref[...])

    def gate_up():
        h = h_ref[...]
        return _dot(h, wg_ref[...]), _dot(h, wu_ref[...])

    @pl.when(j == 0)
    def _():
        shift = mod_ref[0, mod_row:mod_row + 1, :]
        scale = mod_ref[0, mod_row + 1:mod_row + 2, :]
        h_ref[...] = (load_x() * (1.0 + scale) + shift).astype(BF16)
        a, u = gate_up()
        y_ref[...] = (_silu(a) * u).astype(BF16)

    @pl.when(j == 1)
    def _():
        a, u = gate_up()
        acc_ref[...] = _dot(y_ref[...], wd_ref[...])
        y_ref[...] = (_silu(a) * u).astype(BF16)

    @pl.when(jnp.logical_and(j > 1, j < n_tiles))
    def _():
        a, u = gate_up()
        acc_ref[...] += _dot(y_ref[...], wd_ref[...])
        y_ref[...] = (_silu(a) * u).astype(BF16)

    @pl.when(j == n_tiles)
    def _():
        y = acc_ref[...] + _dot(y_ref[...], wd_ref[...])
        gate = mod_ref[0, mod_row + 2:mod_row + 3, :]
        z = alpha * load_x() + (MACARON_WEIGHT * gate) * y
        o_ref[...] = _layernorm(z, g_ref[...], b_ref[...])


def _ffn(x, mod, wg, wu, wd, ln_g, ln_b, *, w_index, mod_row, n_rows, n_batch, n_lat_rows, seq,
         alpha, x_ctx=None):
    d = x.shape[1]
    f = wg.shape[-1]
    tm = _pick(n_rows, (512, 256))
    tf = _pick(f, (512, 256, 128))
    n_tiles = f // tf
    assert n_tiles >= 2
    nlb = n_lat_rows // tm
    split_x = x_ctx is not None
    vmem = ((6 if split_x else 4) * tm * d * 4 + tm * d * 4 + tm * d * 2 + 2 * 3 * d * tf * 2
            + 9 * tm * tf * 4 + 2 * 16 * d * 4)
    idx = functools.partial(_mod_block_index, tm=tm, n_lat_rows=n_lat_rows, seq=seq, n_batch=n_batch)
    if split_x:
        x_specs = [pl.BlockSpec((tm, d), lambda i, j: (jnp.minimum(i, nlb - 1), 0)),
                   pl.BlockSpec((tm, d), lambda i, j: (jnp.maximum(i - nlb, 0), 0))]
        xs = (x, x_ctx)
    else:
        x_specs = [pl.BlockSpec((tm, d), lambda i, j: (i, 0))]
        xs = (x,)
    return pl.pallas_call(
        functools.partial(_ffn_kernel, mod_row=mod_row, alpha=alpha, n_lat_blocks=nlb, split_x=split_x),
        out_shape=jax.ShapeDtypeStruct((n_rows, d), F32),
        grid=(n_rows // tm, n_tiles + 1),
        in_specs=x_specs + [
            pl.BlockSpec((None, 1, N_MOD, d), lambda i, j: (w_index[0], idx(i), 0, 0)),
            pl.BlockSpec((None, None, d, tf), lambda i, j: (*w_index, 0, jnp.minimum(j, n_tiles - 1))),
            pl.BlockSpec((None, None, d, tf), lambda i, j: (*w_index, 0, jnp.minimum(j, n_tiles - 1))),
            pl.BlockSpec((None, None, tf, d), lambda i, j: (*w_index, jnp.maximum(j - 1, 0), 0)),
            pl.BlockSpec((1, d), lambda i, j: (0, 0)),
            pl.BlockSpec((1, d), lambda i, j: (0, 0))],
        out_specs=pl.BlockSpec((tm, d), lambda i, j: (i, 0)),
        scratch_shapes=[pltpu.VMEM((tm, d), BF16), pltpu.VMEM((tm, d), F32), pltpu.VMEM((tm, tf), BF16)],
        compiler_params=_cparams(2, vmem),
        name="ffn_sublayer",
    )(*xs, mod, wg, wu, wd, ln_g, ln_b)


def _inproj_kernel(x_ref, mod_ref, w_ref, wex_ref, p_ref, ex_ref, h_ref, *, mod_row):
    j = pl.program_id(1)

    @pl.when(j == 0)
    def _():
        shift = mod_ref[0, mod_row:mod_row + 1, :]
        scale = mod_ref[0, mod_row + 1:mod_row + 2, :]
        h = (x_ref[...] * (1.0 + scale) + shift).astype(BF16)
        h_ref[...] = h
        ex_ref[...] = _dot(h, wex_ref[...])

    p_ref[...] = _dot(h_ref[...], w_ref[...])


def _inproj(x, mod, w_in, w_ex, *, layer, mod_row, n_batch, n_lat_rows, seq):
    n_rows, d = x.shape
    n = w_in.shape[-1]
    tm = _pick(n_rows, (1024, 512, 256))
    tn = _pick(n, (512, 256, 128))
    vmem = (2 * tm * d * 4 + tm * d * 2 + 2 * d * tn * 2 + 2 * d * EX_WIDTH * 2
            + 4 * tm * tn * 4 + 2 * tm * EX_WIDTH * 4 + 2 * 16 * d * 4 + 2 * tm * d * 4)
    idx = functools.partial(_mod_block_index, tm=tm, n_lat_rows=n_lat_rows, seq=seq, n_batch=n_batch)
    return pl.pallas_call(
        functools.partial(_inproj_kernel, mod_row=mod_row),
        out_shape=(jax.ShapeDtypeStruct((n_rows, n), F32),
                   jax.ShapeDtypeStruct((n_rows, EX_WIDTH), F32)),
        grid=(n_rows // tm, n // tn),
        in_specs=[pl.BlockSpec((tm, d), lambda i, j: (i, 0)),
                  pl.BlockSpec((None, 1, N_MOD, d), lambda i, j: (layer, idx(i), 0, 0)),
                  pl.BlockSpec((None, d, tn), lambda i, j: (layer, 0, j)),
                  pl.BlockSpec((d, EX_WIDTH), lambda i, j: (0, 0))],
        out_specs=(pl.BlockSpec((tm, tn), lambda i, j: (i, j)),
                   pl.BlockSpec((tm, EX_WIDTH), lambda i, j: (i, 0))),
        scratch_shapes=[pltpu.VMEM((tm, d), BF16)],
        compiler_params=_cparams(2, vmem),
        name="mixer_in_proj",
    )(x, mod, w_in, w_ex)


def _prep_kernel(dn_ref, prev_ref, next_ref, conv_ref, aq0_ref, aq1_ref, akv_ref, qg_ref, kg_ref,
                 cos_ref, sin_ref, dn_o, aq_o, ak_o, avt_o, *, n_lat_blocks, blocks_per_seq):
    i = pl.program_id(0)
    is_lat = i < n_lat_blocks
    pos = i % blocks_per_seq
    has_prev = jnp.logical_and(is_lat, pos != 0)
    has_next = jnp.logical_and(is_lat, pos != blocks_per_seq - 1)

    z = dn_ref[...]
    rows = z.shape[0]
    ridx = lax.broadcasted_iota(jnp.int32, z.shape, 0)
    halo_p = jnp.where(has_prev, prev_ref[7:8, :], 0.0)
    halo_n = jnp.where(has_next, next_ref[0:1, :], 0.0)
    z_prev = jnp.where(ridx == 0, halo_p, pltpu.roll(z, 1, 0))
    z_next = jnp.where(ridx == rows - 1, halo_n, pltpu.roll(z, rows - 1, 0))
    y = _silu(z_prev * conv_ref[0:1, :] + z * conv_ref[1:2, :] + z_next * conv_ref[2:3, :])
    for hh in range(2 * DN_HEADS):
        sl = slice(hh * DN_DK, (hh + 1) * DN_DK)
        yh = y[:, sl]
        yn = yh * lax.rsqrt(jnp.sum(yh * yh, axis=-1, keepdims=True) + NORM_EPS)
        if hh < DN_HEADS:
            yn = yn * (DN_DK ** -0.5)
        dn_o[:, sl] = yn
    dn_o[:, 2 * DN_QK_W:] = y[:, 2 * DN_QK_W:]

    lane = lax.broadcasted_iota(jnp.int32, (rows, ATT_HD), 1)
    first_half_of_pair = (lane // (ATT_HD // 4)) % 2 == 0
    cos = cos_ref[...]
    sin = sin_ref[...]

    def rope(xh):
        swapped = jnp.where(first_half_of_pair,
                            pltpu.roll(xh, ATT_HD - ATT_HD // 4, 1), pltpu.roll(xh, ATT_HD // 4, 1))
        return jnp.where(is_lat, xh * cos + swapped * sin, xh)

    for hh in range(ATT_HEADS):
        src = aq0_ref if hh < ATT_HEADS // 2 else aq1_ref
        off = (hh % (ATT_HEADS // 2)) * ATT_HD
        qh = rope(_head_rmsnorm(src[:, off:off + ATT_HD], qg_ref[...])) * (LOG2_E * ATT_HD ** -0.5)
        aq_o[:, hh * ATT_HD:(hh + 1) * ATT_HD] = qh.astype(BF16)
    for hh in range(ATT_KV_HEADS):
        sl = slice(hh * ATT_HD, (hh + 1) * ATT_HD)
        ak_o[:, sl] = rope(_head_rmsnorm(akv_ref[:, sl], kg_ref[...])).astype(BF16)
        vt = akv_ref[:, ATT_KV_W + hh * ATT_HD:ATT_KV_W + (hh + 1) * ATT_HD].T.astype(BF16)
        avt_o[hh * VT_ROWS:hh * VT_ROWS + ATT_HD, :] = vt
        avt_o[hh * VT_ROWS + ATT_HD:(hh + 1) * VT_ROWS, :] = jnp.ones((VT_ROWS - ATT_HD, rows), BF16)


def _prep(p, conv_w, q_norm_g, k_norm_g, rope_cos, rope_sin, *, n_lat_rows, seq):
    n_rows = p.shape[0]
    tm = SEQ_BLOCK
    n_blocks = n_rows // tm
    n_lat_blocks = n_lat_rows // tm
    bps = seq // tm
    dn_w = 2 * DN_QK_W + DN_V_W
    dn_cb = OFF_DN // dn_w
    assert OFF_DN % dn_w == 0 and OFF_ATT % 512 == 0
    att_cb = OFF_ATT // 512
    halo_per_block = tm // 8
    n_halo = n_rows // 8
    vmem = (2 * 2 * tm * dn_w * 4 + 2 * 3 * tm * 512 * 4 + 2 * tm * 1536 * 2
            + 10 * tm * dn_w * 4 + 4 * tm * LANES * 4)
    return pl.pallas_call(
        functools.partial(_prep_kernel, n_lat_blocks=n_lat_blocks, blocks_per_seq=bps),
        out_shape=(jax.ShapeDtypeStruct((n_rows, dn_w), F32),
                   jax.ShapeDtypeStruct((n_rows, ATT_Q_W), BF16),
                   jax.ShapeDtypeStruct((n_rows, ATT_KV_W), BF16),
                   jax.ShapeDtypeStruct((ATT_KV_HEADS * VT_ROWS, n_rows), BF16)),
        grid=(n_blocks,),
        in_specs=[pl.BlockSpec((tm, dn_w), lambda i: (i, dn_cb)),
                  pl.BlockSpec((8, dn_w), lambda i: (jnp.maximum(i * halo_per_block - 1, 0), dn_cb)),
                  pl.BlockSpec((8, dn_w), lambda i: (jnp.minimum((i + 1) * halo_per_block, n_halo - 1), dn_cb)),
                  pl.BlockSpec((DN_CONV, dn_w), lambda i: (0, 0)),
                  pl.BlockSpec((tm, 512), lambda i: (i, att_cb)),
                  pl.BlockSpec((tm, 512), lambda i: (i, att_cb + 1)),
                  pl.BlockSpec((tm, 512), lambda i: (i, att_cb + 2)),
                  pl.BlockSpec((1, ATT_HD), lambda i: (0, 0)),
                  pl.BlockSpec((1, ATT_HD), lambda i: (0, 0)),
                  pl.BlockSpec((tm, ATT_HD), lambda i: (jnp.where(i < n_lat_blocks, i % bps, 0), 0)),
                  pl.BlockSpec((tm, ATT_HD), lambda i: (jnp.where(i < n_lat_blocks, i % bps, 0), 0))],
        out_specs=(pl.BlockSpec((tm, dn_w), lambda i: (i, 0)),
                   pl.BlockSpec((tm, ATT_Q_W), lambda i: (i, 0)),
                   pl.BlockSpec((tm, ATT_KV_W), lambda i: (i, 0)),
                   pl.BlockSpec((ATT_KV_HEADS * VT_ROWS, tm), lambda i: (0, i))),
        compiler_params=_cparams(1, vmem),
        name="mixer_prep",
    )(p, p, p, conv_w, p, p, p, q_norm_g, k_norm_g, rope_cos, rope_sin)


def _seq_block(b, s, *, rev, n_lat_blocks, blocks_per_seq):
    j = (blocks_per_seq - s) if rev else (s - 1)
    return jnp.where(s == 0, n_lat_blocks + b, b * blocks_per_seq + j)


def _tri_masks(rev):
    row = lax.broadcasted_iota(jnp.int32, (CHUNK, CHUNK), 0)
    col = lax.broadcasted_iota(jnp.int32, (CHUNK, CHUNK), 1)
    incl = (row <= col) if rev else (row >= col)
    strict = (row < col) if rev else (row > col)
    return incl, strict


def _gla_kernel(*refs, n_batch):
    n_in = n_batch * 2 * 4
    wa2_ref, ba_ref, of_ref, ob_ref, st_ref = refs[n_in:]
    o_refs = (of_ref, ob_ref)
    streams = [(b, d) + tuple(refs[(b * 2 + d) * 4:(b * 2 + d + 1) * 4])
               for b in range(n_batch) for d in range(2)]

    @pl.when(pl.program_id(0) == 0)
    def _():
        st_ref[...] = jnp.zeros(st_ref.shape, F32)

    masks = [_tri_masks(rev)[0] for rev in (False, True)]
    tris = [jnp.where(m, 1.0, 0.0).astype(BF16) for m in masks]
    units = [(b, d, h) for b in range(n_batch) for d in range(2) for h in range(GLA_HEADS)]
    hs = lambda h: slice(h * GLA_DK, (h + 1) * GLA_DK)
    vs = lambda h: slice(h * GLA_DV, (h + 1) * GLA_DV)

    def chunk_step(c, carry):
        prep = {}
        for b, d, q_ref, k_ref, v_ref, e_ref in streams:
            rev = d == 1
            cc = (CHUNKS_PER_BLOCK - 1 - c) if rev else c
            rows = pl.ds(pl.multiple_of(cc * CHUNK, CHUNK), CHUNK)
            low_rank = e_ref[rows, d * GLA_RANK:(d + 1) * GLA_RANK].astype(BF16)
            logit = _dot(low_rank, wa2_ref[d].astype(BF16)) + ba_ref[d]
            log_a = -_softplus(-logit) / GLA_TAU
            bcum = _cumsum_f32(tris[d], log_a)
            blast = bcum[0:1] if rev else bcum[CHUNK - 1:CHUNK]
            kc = k_ref[rows, :]
            prep[b, d] = dict(
                rows=rows,
                q_in=(q_ref[rows, :] * (GLA_DK ** -0.5) * jnp.exp(bcum)).astype(BF16),
                k_in=(kc * jnp.exp(-bcum)).astype(BF16),
                k_end=(kc * jnp.exp(blast - bcum)).astype(BF16),
                decay=jnp.exp(blast),
                v=v_ref[rows, :].astype(BF16),
                st=st_ref[b, d])
        attn = {(b, d, h): jnp.where(masks[d], _dot_nt(prep[b, d]["q_in"][:, hs(h)],
                                                        prep[b, d]["k_in"][:, hs(h)]), 0.0).astype(BF16)
                for b, d, h in units}
        inter = {(b, d, h): _dot_nt(prep[b, d]["q_in"][:, hs(h)], prep[b, d]["st"][:, hs(h)].astype(BF16))
                 for b, d, h in units}
        intra = {(b, d, h): _dot(attn[b, d, h], prep[b, d]["v"][:, vs(h)]) for b, d, h in units}
        upd = {(b, d, h): _dot_tn(prep[b, d]["v"][:, vs(h)], prep[b, d]["k_end"][:, hs(h)])
               for b, d, h in units}
        for (b, d), pr in prep.items():
            o_refs[d][b, pr["rows"], :] = jnp.concatenate(
                [intra[b, d, h] + inter[b, d, h] for h in range(GLA_HEADS)], axis=1)
            st_ref[b, d] = pr["st"] * pr["decay"] + jnp.concatenate(
                [upd[b, d, h] for h in range(GLA_HEADS)], axis=1)
        return carry

    lax.fori_loop(0, CHUNKS_PER_BLOCK, chunk_step, 0)


def _seq_out_specs(n_batch, tm, width, bps):
    return (pl.BlockSpec((n_batch, tm, width), lambda s: (0, s, 0)),
            pl.BlockSpec((n_batch, tm, width), lambda s: (0, jnp.where(s == 0, 0, bps + 1 - s), 0)))


def _gla(p, ex, wa2, ba, *, n_batch, n_lat_rows, seq):
    tm = SEQ_BLOCK
    bps = seq // tm
    nlb = n_lat_rows // tm
    fwd = functools.partial(_seq_block, rev=False, n_lat_blocks=nlb, blocks_per_seq=bps)
    bwd = functools.partial(_seq_block, rev=True, n_lat_blocks=nlb, blocks_per_seq=bps)
    v_cb = OFF_GLA_V // GLA_V_W

    def specs(b, blk):
        return [pl.BlockSpec((tm, GLA_QK_W), lambda s: (blk(b, s), 0)),
                pl.BlockSpec((tm, GLA_QK_W), lambda s: (blk(b, s), 1)),
                pl.BlockSpec((tm, GLA_V_W), lambda s: (blk(b, s), v_cb)),
                pl.BlockSpec((tm, EX_WIDTH), lambda s: (blk(b, s), 0))]

    n_streams = 2 * n_batch
    vmem = (2 * n_streams * tm * (2 * GLA_QK_W + GLA_V_W + EX_WIDTH) * 4 + 2 * n_streams * tm * GLA_V_W * 4
            + n_streams * GLA_DV * GLA_QK_W * 4 + n_streams * 40 * CHUNK * GLA_QK_W * 4)
    out = jax.ShapeDtypeStruct((n_batch, seq + tm, GLA_V_W), F32)
    return pl.pallas_call(
        functools.partial(_gla_kernel, n_batch=n_batch),
        out_shape=(out, out),
        grid=(bps + 1,),
        in_specs=[spec for b in range(n_batch) for blk in (fwd, bwd) for spec in specs(b, blk)] + [
            pl.BlockSpec((2, GLA_RANK, GLA_QK_W), lambda s: (0, 0, 0)),
            pl.BlockSpec((2, 1, GLA_QK_W), lambda s: (0, 0, 0))],
        out_specs=_seq_out_specs(n_batch, tm, GLA_V_W, bps),
        scratch_shapes=[pltpu.VMEM((n_batch, 2, GLA_DV, GLA_QK_W), F32)],
        compiler_params=_cparams(1, vmem),
        name="gla_scan",
    )(*([p, p, p, ex] * n_streams), wa2, ba)


def _dn_kernel(*refs, n_batch):
    n_in = n_batch * 2 * 2
    alog_ref, dtb_ref, of_ref, ob_ref, st_ref = refs[n_in:]
    o_refs = (of_ref, ob_ref)
    streams = [(b, d) + tuple(refs[(b * 2 + d) * 2:(b * 2 + d + 1) * 2])
               for b in range(n_batch) for d in range(2)]

    @pl.when(pl.program_id(0) == 0)
    def _():
        st_ref[...] = jnp.zeros(st_ref.shape, F32)

    masks = [_tri_masks(rev) for rev in (False, True)]
    tris = [jnp.where(m[0], 1.0, 0.0).astype(BF16) for m in masks]
    row = lax.broadcasted_iota(jnp.int32, (CHUNK, CHUNK), 0)
    col = lax.broadcasted_iota(jnp.int32, (CHUNK, CHUNK), 1)
    eye = jnp.where(row == col, 1.0, 0.0)

    def chunk_step(c, carry):
        units = []
        for b, d, x_ref, e_ref in streams:
            rev = d == 1
            cc = (CHUNKS_PER_BLOCK - 1 - c) if rev else c
            rows = pl.ds(pl.multiple_of(cc * CHUNK, CHUNK), CHUNK)
            lo = EX_DN + 2 * DN_HEADS * d
            a_in = e_ref[rows, lo:lo + DN_HEADS]
            b_in = e_ref[rows, lo + DN_HEADS:lo + 2 * DN_HEADS]
            g = -jnp.exp(alog_ref[d]) * _softplus(a_in + dtb_ref[d])
            beta_all = _sigmoid(b_in)
            gcum_all = _cumsum_f32(tris[d], g)
            for h in range(DN_HEADS):
                gcum = gcum_all[:, h:h + 1]
                glast = gcum[0:1] if rev else gcum[CHUNK - 1:CHUNK]
                gmat = jnp.broadcast_to(gcum, (CHUNK, CHUNK))
                units.append(dict(
                    b=b, d=d, h=h, rows=rows, beta=beta_all[:, h:h + 1], gcum=gcum, glast=glast,
                    decay=jnp.exp(gmat - gmat.T),
                    q=x_ref[rows, h * DN_DK:(h + 1) * DN_DK],
                    k=x_ref[rows, DN_QK_W + h * DN_DK:DN_QK_W + (h + 1) * DN_DK],
                    v=x_ref[rows, 2 * DN_QK_W + h * DN_DV:2 * DN_QK_W + (h + 1) * DN_DV]))
        for u in units:
            u["kb"] = u["k"] * u["beta"]
            u["qk"] = _dot_nt(jnp.concatenate([u["q"], u["kb"]], axis=0).astype(BF16), u["k"].astype(BF16))
        for u in units:
            incl, strict = masks[u["d"]]
            u["attn"] = jnp.where(incl, u["qk"][:CHUNK] * u["decay"], 0.0).astype(BF16)
            n = -jnp.where(strict, u["qk"][CHUNK:] * u["decay"], 0.0)
            u["t"] = eye + n
            u["n"] = n.astype(BF16)
        for u in units:
            u["p"] = _dot(u["n"], u["n"])
        for _ in range(int(math.log2(CHUNK)) - 2):
            for u in units:
                u["both"] = _dot(jnp.concatenate([u["t"], u["p"]], axis=0).astype(BF16), u["p"].astype(BF16))
            for u in units:
                u["t"] = u["t"] + u["both"][:CHUNK]
                u["p"] = u["both"][CHUNK:]
        for u in units:
            u["tp"] = _dot(u["t"].astype(BF16), u["p"].astype(BF16))
        for u in units:
            tinv = (u["t"] + u["tp"]).astype(BF16)
            egc = jnp.exp(u["gcum"])
            rhs = jnp.concatenate([u["v"] * u["beta"], u["kb"] * egc], axis=1).astype(BF16)
            u["uw"] = _dot(tinv, rhs)
            u["q_dec"] = u["q"] * egc
            u["k_end"] = (u["k"] * jnp.exp(u["glast"] - u["gcum"])).astype(BF16)
        for u in units:
            u["state"] = st_ref[u["b"], u["d"], u["h"]]
            lhs = jnp.concatenate([u["uw"][:, DN_DV:], u["q_dec"]], axis=0).astype(BF16)
            u["wq"] = _dot(lhs, u["state"].astype(BF16))
        for u in units:
            u["v_new"] = (u["uw"][:, :DN_DV] - u["wq"][:CHUNK]).astype(BF16)
            u["o"] = u["wq"][CHUNK:] + _dot(u["attn"], u["v_new"])
        for u in units:
            st_ref[u["b"], u["d"], u["h"]] = (u["state"] * jnp.exp(u["glast"])
                                              + _dot_tn(u["k_end"], u["v_new"]))
        for b, d, _, _ in streams:
            sel = [u for u in units if u["b"] == b and u["d"] == d]
            o_refs[d][b, sel[0]["rows"], :] = jnp.concatenate([u["o"] for u in sel], axis=1)
        return carry

    lax.fori_loop(0, CHUNKS_PER_BLOCK, chunk_step, 0)


def _dn(x, ex, a_log, dt_bias, *, n_batch, n_lat_rows, seq):
    w = x.shape[1]
    tm = SEQ_BLOCK
    bps = seq // tm
    nlb = n_lat_rows // tm
    fwd = functools.partial(_seq_block, rev=False, n_lat_blocks=nlb, blocks_per_seq=bps)
    bwd = functools.partial(_seq_block, rev=True, n_lat_blocks=nlb, blocks_per_seq=bps)

    def specs(b, blk):
        return [pl.BlockSpec((tm, w), lambda s: (blk(b, s), 0)),
                pl.BlockSpec((tm, EX_WIDTH), lambda s: (blk(b, s), 0))]

    n_streams = 2 * n_batch
    vmem = (2 * n_streams * tm * (w + EX_WIDTH) * 4 + 2 * n_streams * tm * DN_V_W * 4
            + n_streams * DN_HEADS * DN_DK * DN_DV * 4 + n_streams * DN_HEADS * 40 * CHUNK * DN_DK * 4)
    out = jax.ShapeDtypeStruct((n_batch, seq + tm, DN_V_W), F32)
    return pl.pallas_call(
        functools.partial(_dn_kernel, n_batch=n_batch),
        out_shape=(out, out),
        grid=(bps + 1,),
        in_specs=[spec for b in range(n_batch) for blk in (fwd, bwd) for spec in specs(b, blk)] + [
            pl.BlockSpec((2, 1, DN_HEADS), lambda s: (0, 0, 0)),
            pl.BlockSpec((2, 1, DN_HEADS), lambda s: (0, 0, 0))],
        out_specs=_seq_out_specs(n_batch, tm, DN_V_W, bps),
        scratch_shapes=[pltpu.VMEM((n_batch, 2, DN_HEADS, DN_DK, DN_DV), F32)],
        compiler_params=_cparams(1, vmem),
        name="deltanet_scan",
    )(*([x, ex] * n_streams), a_log, dt_bias)


def _attn_kernel(q_ref, kc_ref, vtc_ref, kl_ref, vtl_ref, o_ref, m_ref, acc_ref,
                 *, with_ctx, tk):
    tq = q_ref.shape[0]
    ctx_chunk = (lambda: kc_ref[...], lambda: vtc_ref[...])
    lat_chunks = [(lambda c=c: kl_ref[c * tk:(c + 1) * tk, :], lambda c=c: vtl_ref[:, c * tk:(c + 1) * tk])
                  for c in range(kl_ref.shape[0] // tk)]

    def scores(chunk, g):
        return _dot_nt(chunk[0](), q_ref[:, g * ATT_HD:(g + 1) * ATT_HD])

    def accumulate(chunk, g, s, first):
        cols = slice(g * tq, (g + 1) * tq)
        m_new = jnp.max(s, axis=0, keepdims=True)
        if not first:
            m_old = m_ref[:, cols]
            m_new = jnp.maximum(m_old, m_new)
            alpha = jnp.exp2(m_old - m_new)
        p = jnp.exp2(s - m_new).astype(BF16)
        pv = _dot(chunk[1](), p)
        if not first:
            pv = alpha * acc_ref[:, cols] + pv
        m_ref[:, cols] = m_new
        acc_ref[:, cols] = pv

    def attend(chunks):
        units = [(ci, g) for ci in range(len(chunks)) for g in range(ATT_GROUP)]
        pending = [scores(chunks[ci], g) for ci, g in units[:ATT_LOOKAHEAD]]
        for i, (ci, g) in enumerate(units):
            if i + ATT_LOOKAHEAD < len(units):
                nci, ng = units[i + ATT_LOOKAHEAD]
                pending.append(scores(chunks[nci], ng))
            accumulate(chunks[ci], g, pending.pop(0), first=ci == 0)
        for g in range(ATT_GROUP):
            cols = slice(g * tq, (g + 1) * tq)
            o = acc_ref[:ATT_HD, cols] / acc_ref[ATT_HD:ATT_HD + 1, cols]
            o_ref[:, g * ATT_HD:(g + 1) * ATT_HD] = o.T

    def latent_query():
        attend([ctx_chunk] + lat_chunks)

    def context_query():
        attend([ctx_chunk])

    if with_ctx:
        is_ctx = pl.program_id(2) == 0
        pl.when(is_ctx)(context_query)
        pl.when(jnp.logical_not(is_ctx))(latent_query)
    else:
        latent_query()


def _attention(aq, ak, avt, *, n_batch, n_lat_rows, seq, ctx_len, with_ctx):
    n_rows = aq.shape[0]
    tq = SEQ_BLOCK
    assert ctx_len == tq
    bps = seq // tq
    nlb = n_lat_rows // tq
    tk = _pick(seq, (512, 256))
    n_q = bps + (1 if with_ctx else 0)
    q_cw = ATT_GROUP * ATT_HD

    def q_blk(b, qi):
        if with_ctx:
            return jnp.where(qi == 0, nlb + b, b * bps + qi - 1)
        return b * bps + qi

    vmem = (2 * tq * q_cw * (2 + 4) + 2 * 2 * (seq + ctx_len) * ATT_HD * 2
            + ATT_GROUP * tq * (2 * 8 + ATT_HD) * 4 + 6 * ATT_GROUP * tq * tk * 4)
    return pl.pallas_call(
        functools.partial(_attn_kernel, with_ctx=with_ctx, tk=tk),
        out_shape=jax.ShapeDtypeStruct((n_rows if with_ctx else n_lat_rows, ATT_Q_W), F32),
        grid=(n_batch, ATT_KV_HEADS, n_q),
        in_specs=[pl.BlockSpec((tq, q_cw), lambda b, h, qi: (q_blk(b, qi), h)),
                  pl.BlockSpec((ctx_len, ATT_HD), lambda b, h, qi: (nlb + b, h)),
                  pl.BlockSpec((VT_ROWS, ctx_len), lambda b, h, qi: (h, nlb + b)),
                  pl.BlockSpec((seq, ATT_HD), lambda b, h, qi: (b, h)),
                  pl.BlockSpec((VT_ROWS, seq), lambda b, h, qi: (h, b))],
        out_specs=pl.BlockSpec((tq, q_cw), lambda b, h, qi: (q_blk(b, qi), h)),
        scratch_shapes=[pltpu.VMEM((1, ATT_GROUP * tq), F32),
                        pltpu.VMEM((VT_ROWS, ATT_GROUP * tq), F32)],
        compiler_params=_cparams(3, vmem),
        name="gqa_attention",
    )(aq, ak, avt, ak, avt)


def _outproj_kernel(x_ref, mod_ref, gf_ref, gb_ref, gr_ref, df_ref, db_ref, dg_ref, at_ref,
                    gn_ref, dn_ref, w_ref, g_ref, b_ref, o_ref, *, mod_row, alpha):
    y = _dot(at_ref[...].astype(BF16), w_ref[GLA_V_W + DN_V_W:, :])
    row0 = 0
    for f_ref, b_ref_, gate_ref, norm_ref, heads, dv in ((gf_ref, gb_ref, gr_ref, gn_ref, GLA_HEADS, GLA_DV),
                                                         (df_ref, db_ref, dg_ref, dn_ref, DN_HEADS, DN_DV)):
        parts = []
        for h in range(heads):
            sl = slice(h * dv, (h + 1) * dv)
            o = _head_rmsnorm(f_ref[0, :, sl] + b_ref_[0, :, sl], norm_ref[...]) * _silu(gate_ref[:, sl])
            parts.append(o.astype(BF16))
        y = y + _dot(jnp.concatenate(parts, axis=1), w_ref[row0:row0 + heads * dv, :])
        row0 += heads * dv
    gate = mod_ref[0, mod_row:mod_row + 1, :]
    o_ref[...] = _layernorm(alpha * x_ref[...] + gate * y, g_ref[...], b_ref[...])


def _outproj(x, mod, gla_f, gla_b, p, dn_f, dn_b, att, gla_norm_g, dn_norm_g, w_out, ln_g, ln_b,
             *, layer, mod_row, n_rows, n_batch, n_lat_rows, seq, alpha):
    d = x.shape[1]
    tm = SEQ_BLOCK
    idx = functools.partial(_mod_block_index, tm=tm, n_lat_rows=n_lat_rows, seq=seq, n_batch=n_batch)
    row = lambda w, cb: pl.BlockSpec((tm, w), lambda i: (i, cb))
    vec = lambda w: pl.BlockSpec((1, w), lambda i: (0, 0))
    bps = seq // tm
    nlb = n_lat_rows // tm
    scan = lambda w: pl.BlockSpec((1, tm, w), lambda i: (jnp.where(i < nlb, i // bps, i - nlb),
                                                         jnp.where(i < nlb, 1 + i % bps, 0), 0))
    vmem = (2 * tm * (2 * d + 6 * GLA_V_W + ATT_Q_W) * 4 + 2 * MIX_WIDTH * d * 2
            + tm * MIX_WIDTH * 2 + 6 * tm * d * 4)
    return pl.pallas_call(
        functools.partial(_outproj_kernel, mod_row=mod_row, alpha=alpha),
        out_shape=jax.ShapeDtypeStruct((n_rows, d), F32),
        grid=(n_rows // tm,),
        in_specs=[row(d, 0),
                  pl.BlockSpec((None, 1, N_MOD, d), lambda i: (layer, idx(i), 0, 0)),
                  scan(GLA_V_W), scan(GLA_V_W), row(GLA_V_W, OFF_GLA_R // GLA_V_W),
                  scan(DN_V_W), scan(DN_V_W), row(DN_V_W, OFF_DN_GATE // DN_V_W),
                  row(ATT_Q_W, 0),
                  vec(GLA_DV), vec(DN_DV),
                  pl.BlockSpec((None, MIX_WIDTH, d), lambda i: (layer, 0, 0)),
                  vec(d), vec(d)],
        out_specs=row(d, 0),
        compiler_params=_cparams(1, vmem),
        name="mixer_out_proj",
    )(x, mod, gla_f, gla_b, p, dn_f, dn_b, p, att, gla_norm_g, dn_norm_g, w_out, ln_g, ln_b)


def _rope_tables(seq):
    t = jnp.arange(seq)
    pr = (t // GRID_W).astype(F32)
    pc = (t % GRID_W).astype(F32)
    axis_dim = ATT_HD // 2
    inv = ROPE_THETA ** (-jnp.arange(0, axis_dim, 2, dtype=F32) / axis_dim)
    ar = pr[:, None] * inv
    ac = pc[:, None] * inv
    cos = jnp.concatenate([jnp.cos(ar), jnp.cos(ar), jnp.cos(ac), jnp.cos(ac)], axis=1)
    sin = jnp.concatenate([-jnp.sin(ar), jnp.sin(ar), -jnp.sin(ac), jnp.sin(ac)], axis=1)
    return cos, sin


def kernel(x, c, ctx, c_ctx, w_ada, b_ada, ln_g, ln_b, w_ffn_gate, w_ffn_up, w_ffn_down, w_in,
           gla_wa1, gla_wa2, gla_ba, gla_norm_g, dn_conv, dn_wab, dn_a_log, dn_dt_bias, dn_norm_g,
           q_norm_g, k_norm_g, w_out):
    n_batch, seq, d = x.shape
    ctx_len = ctx.shape[1]
    depth = w_ada.shape[0]
    alpha = (2 * depth) ** 0.25
    n_lat = n_batch * seq
    n_all = n_lat + n_batch * ctx_len
    assert n_batch + 1 <= 8 and seq % SEQ_BLOCK == 0 and ctx_len == SEQ_BLOCK

    cond = jnp.zeros((8, d), F32).at[:n_batch].set(c).at[n_batch].set(c_ctx)
    mod_all = _ada_table(cond, w_ada, b_ada).reshape(depth, 8, N_MOD, d)
    rope_cos, rope_sin = _rope_tables(seq)
    xs = x.reshape(n_lat, d)
    x_ctx = ctx.reshape(n_batch * ctx_len, d)
    ffn_w = (w_ffn_gate.astype(BF16), w_ffn_up.astype(BF16), w_ffn_down.astype(BF16))
    w_in_b = w_in.astype(BF16)
    w_out_b = w_out.astype(BF16)

    for layer in range(depth):
        last = layer == depth - 1
        mod = mod_all
        lg = ln_g[layer].reshape(3, 1, d)
        lb = ln_b[layer].reshape(3, 1, d)
        w_ex = jnp.concatenate([gla_wa1[layer, 0], gla_wa1[layer, 1], dn_wab[layer, 0], dn_wab[layer, 1]], axis=1)
        w_ex = jnp.pad(w_ex, ((0, 0), (0, EX_WIDTH - w_ex.shape[1]))).astype(BF16)
        common = dict(n_lat_rows=n_lat, seq=seq)

        xs = _ffn(xs, mod, *ffn_w, lg[0], lb[0], w_index=(layer, 0), mod_row=0, n_rows=n_all, alpha=alpha,
                  x_ctx=x_ctx if layer == 0 else None, n_batch=n_batch, **common)
        p, ex = _inproj(xs, mod, w_in_b, w_ex, layer=layer, mod_row=3, n_batch=n_batch, **common)
        dn_x, aq, ak, avt = _prep(p, dn_conv[layer], q_norm_g[layer].reshape(1, ATT_HD),
                              k_norm_g[layer].reshape(1, ATT_HD), rope_cos, rope_sin, **common)
        gla_f, gla_b = _gla(p, ex, gla_wa2[layer], gla_ba[layer].reshape(2, 1, GLA_QK_W),
                            n_batch=n_batch, **common)
        dn_f, dn_b = _dn(dn_x, ex, dn_a_log[layer].reshape(2, 1, DN_HEADS),
                         dn_dt_bias[layer].reshape(2, 1, DN_HEADS), n_batch=n_batch, **common)
        att = _attention(aq, ak, avt, n_batch=n_batch, ctx_len=ctx_len, with_ctx=not last, **common)
        n_out = n_lat if last else n_all
        xs = _outproj(xs, mod, gla_f, gla_b, p, dn_f, dn_b, att, gla_norm_g[layer].reshape(1, GLA_DV),
                      dn_norm_g[layer].reshape(1, DN_DV), w_out_b, lg[1], lb[1],
                      layer=layer, mod_row=5, n_rows=n_out, alpha=alpha, n_batch=n_batch, **common)
        xs = _ffn(xs, mod, *ffn_w, lg[2], lb[2], w_index=(layer, 1), mod_row=6, n_rows=n_out, alpha=alpha,
                  n_batch=n_batch, **common)
    return xs.reshape(n_batch, seq, d)
```

```python
import functools
import math

import jax
import jax.numpy as jnp
from jax import lax
from jax.experimental import pallas as pl
from jax.experimental.pallas import tpu as pltpu

F32 = jnp.float32
BF16 = jnp.bfloat16

N_MOD = 9
GRID_W = 64
MACARON_WEIGHT = 0.5
GLA_HEADS, GLA_DK, GLA_DV, GLA_RANK, GLA_TAU = 4, 64, 128, 16, 16.0
DN_HEADS, DN_DK, DN_DV, DN_CONV = 4, 128, 128, 3
ATT_HEADS, ATT_KV_HEADS, ATT_HD = 8, 2, 128
ATT_GROUP = ATT_HEADS // ATT_KV_HEADS
VT_ROWS = ATT_HD + 16
ATT_LOOKAHEAD = 3
CHUNK = 64
ROPE_THETA = 10000.0
NORM_EPS = 1e-6
LOG2_E = math.log2(math.e)
GLA_QK_W, GLA_V_W = GLA_HEADS * GLA_DK, GLA_HEADS * GLA_DV
DN_QK_W, DN_V_W = DN_HEADS * DN_DK, DN_HEADS * DN_DV
ATT_Q_W, ATT_KV_W = ATT_HEADS * ATT_HD, ATT_KV_HEADS * ATT_HD
IN_WIDTH = 2 * GLA_QK_W + 2 * GLA_V_W + 2 * DN_QK_W + 2 * DN_V_W + ATT_Q_W + 2 * ATT_KV_W
MIX_WIDTH = GLA_V_W + DN_V_W + ATT_Q_W
OFF_GLA_V, OFF_GLA_R = 2 * GLA_QK_W, 2 * GLA_QK_W + GLA_V_W
OFF_DN = OFF_GLA_R + GLA_V_W
OFF_DN_GATE = OFF_DN + 2 * DN_QK_W + DN_V_W
OFF_ATT = OFF_DN_GATE + DN_V_W
EX_WIDTH = 128
EX_DN = 2 * GLA_RANK

V7X_VMEM_BYTES = 64 * 1024 * 1024
LANES = 128
SEQ_BLOCK = 256
CHUNKS_PER_BLOCK = SEQ_BLOCK // CHUNK
HALO_ROWS = 16


def _cparams(n_axes, vmem_bytes):
    limit = int(min(max(vmem_bytes, 16 * 1024 * 1024), V7X_VMEM_BYTES - 8 * 1024 * 1024))
    return pltpu.CompilerParams(dimension_semantics=("arbitrary",) * n_axes,
                                vmem_limit_bytes=limit)


def _dot(a, b):
    return jnp.dot(a, b, preferred_element_type=F32)


def _dot_nt(a, b):
    return lax.dot_general(a, b, (((1,), (1,)), ((), ())), preferred_element_type=F32)


def _dot_tn(a, b):
    return lax.dot_general(a, b, (((0,), (0,)), ((), ())), preferred_element_type=F32)


def _sigmoid(x):
    return 1.0 / (1.0 + jnp.exp(-x))


def _silu(x):
    return x * _sigmoid(x)


def _softplus(x):
    return jnp.maximum(x, 0.0) + jnp.log(1.0 + jnp.exp(-jnp.abs(x)))


def _layernorm(z, g, b):
    mu = jnp.mean(z, axis=-1, keepdims=True)
    zc = z - mu
    var = jnp.mean(zc * zc, axis=-1, keepdims=True)
    return zc * lax.rsqrt(var + NORM_EPS) * g + b


def _head_rmsnorm(x, g):
    return x * lax.rsqrt(jnp.mean(x * x, axis=-1, keepdims=True) + NORM_EPS) * g


def _cumsum_f32(tri, g):
    g1 = g.astype(BF16)
    r1 = g - g1.astype(F32)
    g2 = r1.astype(BF16)
    g3 = (r1 - g2.astype(F32)).astype(BF16)
    return _dot(tri, g1) + _dot(tri, g2) + _dot(tri, g3)


def _pick(n, candidates):
    for c in candidates:
        if n % c == 0:
            return c
    raise ValueError(f"no block size in {candidates} divides {n}")


def _mod_block_index(i, tm, n_lat_rows, seq, n_batch):
    assert seq % tm == 0 and n_lat_rows % tm == 0
    r = i * tm
    return jnp.where(r < n_lat_rows, r // seq, n_batch)


def _ada_kernel(c_ref, w_ref, b_ref, o_ref):
    s = _silu(c_ref[...]).astype(BF16)
    o_ref[0] = _dot(s, w_ref[0].astype(BF16)) + b_ref[0]


def _ada_table(cond, w_ada, b_ada):
    n_layer, d, n = w_ada.shape
    tn = _pick(n, (1024, 512, 256, 128))
    vmem = 2 * d * tn * 4 + d * tn * 2 + 4 * 8 * (d + 2 * tn) * 4
    return pl.pallas_call(
        _ada_kernel,
        out_shape=jax.ShapeDtypeStruct((n_layer, 8, n), F32),
        grid=(n_layer, n // tn),
        in_specs=[pl.BlockSpec((8, d), lambda l, j: (0, 0)),
                  pl.BlockSpec((1, d, tn), lambda l, j: (l, 0, j)),
                  pl.BlockSpec((1, 1, tn), lambda l, j: (l, 0, j))],
        out_specs=pl.BlockSpec((1, 8, tn), lambda l, j: (l, 0, j)),
        compiler_params=_cparams(2, vmem),
        name="ada_table",
    )(cond, w_ada, b_ada.reshape(n_layer, 1, n))


def _ffn_kernel(*refs, mod_row, alpha, n_lat_blocks, split_x):
    if split_x:
        x_ref, xc_ref, *refs = refs
    else:
        x_ref, *refs = refs
    mod_ref, wg_ref, wu_ref, wd_ref, g_ref, b_ref, o_ref, h_ref, acc_ref = refs
    j = pl.program_id(1)

    def load_x():
        if not split_x:
            return x_ref[...]
        return jnp.where(pl.program_id(0) < n_lat_blocks, x_ref[...], xc_ref[...])

    @pl.when(j == 0)
    def _():
        shift = mod_ref[0, mod_row:mod_row + 1, :]
        scale = mod_ref[0, mod_row + 1:mod_row + 2, :]
        h_ref[...] = (load_x() * (1.0 + scale) + shift).astype(BF16)
        acc_ref[...] = jnp.zeros(acc_ref.shape, F32)

    h = h_ref[...]
    a = _dot(h, wg_ref[...])
    u = _dot(h, wu_ref[...])
    acc_ref[...] += _dot((_silu(a) * u).astype(BF16), wd_ref[...])

    @pl.when(j == pl.num_programs(1) - 1)
    def _():
        gate = mod_ref[0, mod_row + 2:mod_row + 3, :]
        z = alpha * load_x() + (MACARON_WEIGHT * gate) * acc_ref[...]
        o_ref[...] = _layernorm(z, g_ref[...], b_ref[...])


def _ffn(x, mod, wg, wu, wd, ln_g, ln_b, *, w_index, mod_row, n_rows, n_batch, n_lat_rows, seq,
         alpha, x_ctx=None):
    d = x.shape[1]
    f = wg.shape[-1]
    tm = _pick(n_rows, (512, 256))
    tf = _pick(f, (512, 256, 128))
    nlb = n_lat_rows // tm
    split_x = x_ctx is not None
    vmem = ((6 if split_x else 4) * tm * d * 4 + tm * d * 4 + tm * d * 2 + 2 * 3 * d * tf * 2
            + 8 * tm * tf * 4 + 2 * 16 * d * 4)
    idx = functools.partial(_mod_block_index, tm=tm, n_lat_rows=n_lat_rows, seq=seq, n_batch=n_batch)
    if split_x:
        x_specs = [pl.BlockSpec((tm, d), lambda i, j: (jnp.minimum(i, nlb - 1), 0)),
                   pl.BlockSpec((tm, d), lambda i, j: (jnp.maximum(i - nlb, 0), 0))]
        xs = (x, x_ctx)
    else:
        x_specs = [pl.BlockSpec((tm, d), lambda i, j: (i, 0))]
        xs = (x,)
    return pl.pallas_call(
        functools.partial(_ffn_kernel, mod_row=mod_row, alpha=alpha, n_lat_blocks=nlb, split_x=split_x),
        out_shape=jax.ShapeDtypeStruct((n_rows, d), F32),
        grid=(n_rows // tm, f // tf),
        in_specs=x_specs + [
            pl.BlockSpec((None, 1, N_MOD, d), lambda i, j: (w_index[0], idx(i), 0, 0)),
            pl.BlockSpec((None, None, d, tf), lambda i, j: (*w_index, 0, j)),
            pl.BlockSpec((None, None, d, tf), lambda i, j: (*w_index, 0, j)),
            pl.BlockSpec((None, None, tf, d), lambda i, j: (*w_index, j, 0)),
            pl.BlockSpec((1, d), lambda i, j: (0, 0)),
            pl.BlockSpec((1, d), lambda i, j: (0, 0))],
        out_specs=pl.BlockSpec((tm, d), lambda i, j: (i, 0)),
        scratch_shapes=[pltpu.VMEM((tm, d), BF16), pltpu.VMEM((tm, d), F32)],
        compiler_params=_cparams(2, vmem),
        name="ffn_sublayer",
    )(*xs, mod, wg, wu, wd, ln_g, ln_b)


def _inproj_kernel(x_ref, mod_ref, w_ref, wex_ref, p_ref, ex_ref, h_ref, *, mod_row):
    j = pl.program_id(1)

    @pl.when(j == 0)
    def _():
        shift = mod_ref[0, mod_row:mod_row + 1, :]
        scale = mod_ref[0, mod_row + 1:mod_row + 2, :]
        h = (x_ref[...] * (1.0 + scale) + shift).astype(BF16)
        h_ref[...] = h
        ex_ref[...] = _dot(h, wex_ref[...])

    p_ref[...] = _dot(h_ref[...], w_ref[...]).astype(p_ref.dtype)


def _inproj(x, mod, w_in, w_ex, *, layer, mod_row, n_batch, n_lat_rows, seq):
    n_rows, d = x.shape
    n = w_in.shape[-1]
    tm = _pick(n_rows, (1024, 512, 256))
    tn = _pick(n, (1280, 512, 256, 128))
    vmem = (2 * tm * d * 4 + tm * d * 2 + 2 * d * tn * 2 + 2 * d * EX_WIDTH * 2
            + 2 * tm * tn * 2 + 2 * tm * tn * 4 + 2 * tm * EX_WIDTH * 4 + 2 * 16 * d * 4 + 2 * tm * d * 4)
    idx = functools.partial(_mod_block_index, tm=tm, n_lat_rows=n_lat_rows, seq=seq, n_batch=n_batch)
    return pl.pallas_call(
        functools.partial(_inproj_kernel, mod_row=mod_row),
        out_shape=(jax.ShapeDtypeStruct((n_rows, n), BF16),
                   jax.ShapeDtypeStruct((n_rows, EX_WIDTH), F32)),
        grid=(n_rows // tm, n // tn),
        in_specs=[pl.BlockSpec((tm, d), lambda i, j: (i, 0)),
                  pl.BlockSpec((None, 1, N_MOD, d), lambda i, j: (layer, idx(i), 0, 0)),
                  pl.BlockSpec((None, d, tn), lambda i, j: (layer, 0, j)),
                  pl.BlockSpec((d, EX_WIDTH), lambda i, j: (0, 0))],
        out_specs=(pl.BlockSpec((tm, tn), lambda i, j: (i, j)),
                   pl.BlockSpec((tm, EX_WIDTH), lambda i, j: (i, 0))),
        scratch_shapes=[pltpu.VMEM((tm, d), BF16)],
        compiler_params=_cparams(2, vmem),
        name="mixer_in_proj",
    )(x, mod, w_in, w_ex)


def _prep_kernel(dn_ref, prev_ref, next_ref, conv_ref, aq0_ref, aq1_ref, akv_ref, qg_ref, kg_ref,
                 cos_ref, sin_ref, dn_o, aq_o, ak_o, avt_o, *, n_lat_blocks, blocks_per_seq):
    i = pl.program_id(0)
    is_lat = i < n_lat_blocks
    pos = i % blocks_per_seq
    has_prev = jnp.logical_and(is_lat, pos != 0)
    has_next = jnp.logical_and(is_lat, pos != blocks_per_seq - 1)

    z = dn_ref[...].astype(F32)
    rows = z.shape[0]
    ridx = lax.broadcasted_iota(jnp.int32, z.shape, 0)
    halo_p = jnp.where(has_prev, prev_ref[HALO_ROWS - 1:HALO_ROWS, :].astype(F32), 0.0)
    halo_n = jnp.where(has_next, next_ref[0:1, :].astype(F32), 0.0)
    z_prev = jnp.where(ridx == 0, halo_p, pltpu.roll(z, 1, 0))
    z_next = jnp.where(ridx == rows - 1, halo_n, pltpu.roll(z, rows - 1, 0))
    y = _silu(z_prev * conv_ref[0:1, :] + z * conv_ref[1:2, :] + z_next * conv_ref[2:3, :])
    for hh in range(2 * DN_HEADS):
        sl = slice(hh * DN_DK, (hh + 1) * DN_DK)
        yh = y[:, sl]
        yn = yh * lax.rsqrt(jnp.sum(yh * yh, axis=-1, keepdims=True) + NORM_EPS)
        if hh < DN_HEADS:
            yn = yn * (DN_DK ** -0.5)
        dn_o[:, sl] = yn
    dn_o[:, 2 * DN_QK_W:] = y[:, 2 * DN_QK_W:]

    lane = lax.broadcasted_iota(jnp.int32, (rows, ATT_HD), 1)
    first_half_of_pair = (lane // (ATT_HD // 4)) % 2 == 0
    cos = cos_ref[...]
    sin = sin_ref[...]

    def rope(xh):
        swapped = jnp.where(first_half_of_pair,
                            pltpu.roll(xh, ATT_HD - ATT_HD // 4, 1), pltpu.roll(xh, ATT_HD // 4, 1))
        return jnp.where(is_lat, xh * cos + swapped * sin, xh)

    for hh in range(ATT_HEADS):
        src = aq0_ref if hh < ATT_HEADS // 2 else aq1_ref
        off = (hh % (ATT_HEADS // 2)) * ATT_HD
        qh = rope(_head_rmsnorm(src[:, off:off + ATT_HD].astype(F32), qg_ref[...])) * (LOG2_E * ATT_HD ** -0.5)
        aq_o[:, hh * ATT_HD:(hh + 1) * ATT_HD] = qh.astype(BF16)
    for hh in range(ATT_KV_HEADS):
        sl = slice(hh * ATT_HD, (hh + 1) * ATT_HD)
        ak_o[:, sl] = rope(_head_rmsnorm(akv_ref[:, sl].astype(F32), kg_ref[...])).astype(BF16)
        vt = akv_ref[:, ATT_KV_W + hh * ATT_HD:ATT_KV_W + (hh + 1) * ATT_HD].astype(F32).T.astype(BF16)
        avt_o[hh * VT_ROWS:hh * VT_ROWS + ATT_HD, :] = vt
        avt_o[hh * VT_ROWS + ATT_HD:(hh + 1) * VT_ROWS, :] = jnp.ones((VT_ROWS - ATT_HD, rows), BF16)


def _prep(p, conv_w, q_norm_g, k_norm_g, rope_cos, rope_sin, *, n_lat_rows, seq):
    n_rows = p.shape[0]
    tm = SEQ_BLOCK
    n_blocks = n_rows // tm
    n_lat_blocks = n_lat_rows // tm
    bps = seq // tm
    dn_w = 2 * DN_QK_W + DN_V_W
    dn_cb = OFF_DN // dn_w
    assert OFF_DN % dn_w == 0 and OFF_ATT % 512 == 0
    att_cb = OFF_ATT // 512
    halo_per_block = tm // HALO_ROWS
    n_halo = n_rows // HALO_ROWS
    vmem = (2 * 2 * tm * dn_w * 4 + 2 * 3 * tm * 512 * 4 + 2 * tm * 1536 * 2
            + 10 * tm * dn_w * 4 + 4 * tm * LANES * 4)
    return pl.pallas_call(
        functools.partial(_prep_kernel, n_lat_blocks=n_lat_blocks, blocks_per_seq=bps),
        out_shape=(jax.ShapeDtypeStruct((n_rows, dn_w), F32),
                   jax.ShapeDtypeStruct((n_rows, ATT_Q_W), BF16),
                   jax.ShapeDtypeStruct((n_rows, ATT_KV_W), BF16),
                   jax.ShapeDtypeStruct((ATT_KV_HEADS * VT_ROWS, n_rows), BF16)),
        grid=(n_blocks,),
        in_specs=[pl.BlockSpec((tm, dn_w), lambda i: (i, dn_cb)),
                  pl.BlockSpec((HALO_ROWS, dn_w), lambda i: (jnp.maximum(i * halo_per_block - 1, 0), dn_cb)),
                  pl.BlockSpec((HALO_ROWS, dn_w),
                               lambda i: (jnp.minimum((i + 1) * halo_per_block, n_halo - 1), dn_cb)),
                  pl.BlockSpec((DN_CONV, dn_w), lambda i: (0, 0)),
                  pl.BlockSpec((tm, 512), lambda i: (i, att_cb)),
                  pl.BlockSpec((tm, 512), lambda i: (i, att_cb + 1)),
                  pl.BlockSpec((tm, 512), lambda i: (i, att_cb + 2)),
                  pl.BlockSpec((1, ATT_HD), lambda i: (0, 0)),
                  pl.BlockSpec((1, ATT_HD), lambda i: (0, 0)),
                  pl.BlockSpec((tm, ATT_HD), lambda i: (jnp.where(i < n_lat_blocks, i % bps, 0), 0)),
                  pl.BlockSpec((tm, ATT_HD), lambda i: (jnp.where(i < n_lat_blocks, i % bps, 0), 0))],
        out_specs=(pl.BlockSpec((tm, dn_w), lambda i: (i, 0)),
                   pl.BlockSpec((tm, ATT_Q_W), lambda i: (i, 0)),
                   pl.BlockSpec((tm, ATT_KV_W), lambda i: (i, 0)),
                   pl.BlockSpec((ATT_KV_HEADS * VT_ROWS, tm), lambda i: (0, i))),
        compiler_params=_cparams(1, vmem),
        name="mixer_prep",
    )(p, p, p, conv_w, p, p, p, q_norm_g, k_norm_g, rope_cos, rope_sin)


def _seq_block(b, s, *, rev, n_lat_blocks, blocks_per_seq):
    j = (blocks_per_seq - s) if rev else (s - 1)
    return jnp.where(s == 0, n_lat_blocks + b, b * blocks_per_seq + j)


def _tri_masks(rev):
    row = lax.broadcasted_iota(jnp.int32, (CHUNK, CHUNK), 0)
    col = lax.broadcasted_iota(jnp.int32, (CHUNK, CHUNK), 1)
    incl = (row <= col) if rev else (row >= col)
    strict = (row < col) if rev else (row > col)
    return incl, strict


def _gla_kernel(*refs, n_batch):
    n_in = n_batch * 2 * 4
    wa2_ref, ba_ref, of_ref, ob_ref, st_ref = refs[n_in:]
    o_refs = (of_ref, ob_ref)
    streams = [(b, d) + tuple(refs[(b * 2 + d) * 4:(b * 2 + d + 1) * 4])
               for b in range(n_batch) for d in range(2)]

    @pl.when(pl.program_id(0) == 0)
    def _():
        st_ref[...] = jnp.zeros(st_ref.shape, F32)

    masks = [_tri_masks(rev)[0] for rev in (False, True)]
    tris = [jnp.where(m, 1.0, 0.0).astype(BF16) for m in masks]
    units = [(b, d, h) for b in range(n_batch) for d in range(2) for h in range(GLA_HEADS)]
    hs = lambda h: slice(h * GLA_DK, (h + 1) * GLA_DK)
    vs = lambda h: slice(h * GLA_DV, (h + 1) * GLA_DV)

    def chunk_step(c, carry):
        prep = {}
        for b, d, q_ref, k_ref, v_ref, e_ref in streams:
            rev = d == 1
            cc = (CHUNKS_PER_BLOCK - 1 - c) if rev else c
            rows = pl.ds(pl.multiple_of(cc * CHUNK, CHUNK), CHUNK)
            low_rank = e_ref[rows, d * GLA_RANK:(d + 1) * GLA_RANK].astype(BF16)
            logit = _dot(low_rank, wa2_ref[d].astype(BF16)) + ba_ref[d]
            log_a = -_softplus(-logit) / GLA_TAU
            bcum = _cumsum_f32(tris[d], log_a)
            blast = bcum[0:1] if rev else bcum[CHUNK - 1:CHUNK]
            kc = k_ref[rows, :].astype(F32)
            prep[b, d] = dict(
                rows=rows,
                q_in=(q_ref[rows, :].astype(F32) * (GLA_DK ** -0.5) * jnp.exp(bcum)).astype(BF16),
                k_in=(kc * jnp.exp(-bcum)).astype(BF16),
                k_end=(kc * jnp.exp(blast - bcum)).astype(BF16),
                decay=jnp.exp(blast),
                v=v_ref[rows, :].astype(BF16),
                st=st_ref[b, d])
        attn = {(b, d, h): jnp.where(masks[d], _dot_nt(prep[b, d]["q_in"][:, hs(h)],
                                                        prep[b, d]["k_in"][:, hs(h)]), 0.0).astype(BF16)
                for b, d, h in units}
        inter = {(b, d, h): _dot_nt(prep[b, d]["q_in"][:, hs(h)], prep[b, d]["st"][:, hs(h)].astype(BF16))
                 for b, d, h in units}
        intra = {(b, d, h): _dot(attn[b, d, h], prep[b, d]["v"][:, vs(h)]) for b, d, h in units}
        upd = {(b, d, h): _dot_tn(prep[b, d]["v"][:, vs(h)], prep[b, d]["k_end"][:, hs(h)])
               for b, d, h in units}
        for (b, d), pr in prep.items():
            o_refs[d][b, pr["rows"], :] = jnp.concatenate(
                [intra[b, d, h] + inter[b, d, h] for h in range(GLA_HEADS)], axis=1)
            st_ref[b, d] = pr["st"] * pr["decay"] + jnp.concatenate(
                [upd[b, d, h] for h in range(GLA_HEADS)], axis=1)
        return carry

    lax.fori_loop(0, CHUNKS_PER_BLOCK, chunk_step, 0)


def _seq_out_specs(n_batch, tm, width, bps):
    return (pl.BlockSpec((n_batch, tm, width), lambda s: (0, s, 0)),
            pl.BlockSpec((n_batch, tm, width), lambda s: (0, jnp.where(s == 0, 0, bps + 1 - s), 0)))


def _gla(p, ex, wa2, ba, *, n_batch, n_lat_rows, seq):
    tm = SEQ_BLOCK
    bps = seq // tm
    nlb = n_lat_rows // tm
    fwd = functools.partial(_seq_block, rev=False, n_lat_blocks=nlb, blocks_per_seq=bps)
    bwd = functools.partial(_seq_block, rev=True, n_lat_blocks=nlb, blocks_per_seq=bps)
    v_cb = OFF_GLA_V // GLA_V_W

    def specs(b, blk):
        return [pl.BlockSpec((tm, GLA_QK_W), lambda s: (blk(b, s), 0)),
                pl.BlockSpec((tm, GLA_QK_W), lambda s: (blk(b, s), 1)),
                pl.BlockSpec((tm, GLA_V_W), lambda s: (blk(b, s), v_cb)),
                pl.BlockSpec((tm, EX_WIDTH), lambda s: (blk(b, s), 0))]

    n_streams = 2 * n_batch
    vmem = (2 * n_streams * tm * (2 * GLA_QK_W + GLA_V_W + EX_WIDTH) * 4 + 2 * n_streams * tm * GLA_V_W * 4
            + n_streams * GLA_DV * GLA_QK_W * 4 + n_streams * 40 * CHUNK * GLA_QK_W * 4)
    out = jax.ShapeDtypeStruct((n_batch, seq + tm, GLA_V_W), F32)
    return pl.pallas_call(
        functools.partial(_gla_kernel, n_batch=n_batch),
        out_shape=(out, out),
        grid=(bps + 1,),
        in_specs=[spec for b in range(n_batch) for blk in (fwd, bwd) for spec in specs(b, blk)] + [
            pl.BlockSpec((2, GLA_RANK, GLA_QK_W), lambda s: (0, 0, 0)),
            pl.BlockSpec((2, 1, GLA_QK_W), lambda s: (0, 0, 0))],
        out_specs=_seq_out_specs(n_batch, tm, GLA_V_W, bps),
        scratch_shapes=[pltpu.VMEM((n_batch, 2, GLA_DV, GLA_QK_W), F32)],
        compiler_params=_cparams(1, vmem),
        name="gla_scan",
    )(*([p, p, p, ex] * n_streams), wa2, ba)


def _dn_kernel(*refs, n_batch):
    n_in = n_batch * 2 * 2
    alog_ref, dtb_ref, of_ref, ob_ref, st_ref = refs[n_in:]
    o_refs = (of_ref, ob_ref)
    streams = [(b, d) + tuple(refs[(b * 2 + d) * 2:(b * 2 + d + 1) * 2])
               for b in range(n_batch) for d in range(2)]

    @pl.when(pl.program_id(0) == 0)
    def _():
        st_ref[...] = jnp.zeros(st_ref.shape, F32)

    masks = [_tri_masks(rev) for rev in (False, True)]
    tris = [jnp.where(m[0], 1.0, 0.0).astype(BF16) for m in masks]
    row = lax.broadcasted_iota(jnp.int32, (CHUNK, CHUNK), 0)
    col = lax.broadcasted_iota(jnp.int32, (CHUNK, CHUNK), 1)
    eye = jnp.where(row == col, 1.0, 0.0)

    def chunk_step(c, carry):
        units = []
        for b, d, x_ref, e_ref in streams:
            rev = d == 1
            cc = (CHUNKS_PER_BLOCK - 1 - c) if rev else c
            rows = pl.ds(pl.multiple_of(cc * CHUNK, CHUNK), CHUNK)
            lo = EX_DN + 2 * DN_HEADS * d
            a_in = e_ref[rows, lo:lo + DN_HEADS]
            b_in = e_ref[rows, lo + DN_HEADS:lo + 2 * DN_HEADS]
            g = -jnp.exp(alog_ref[d]) * _softplus(a_in + dtb_ref[d])
            beta_all = _sigmoid(b_in)
            gcum_all = _cumsum_f32(tris[d], g)
            for h in range(DN_HEADS):
                gcum = gcum_all[:, h:h + 1]
                glast = gcum[0:1] if rev else gcum[CHUNK - 1:CHUNK]
                gmat = jnp.broadcast_to(gcum, (CHUNK, CHUNK))
                units.append(dict(
                    b=b, d=d, h=h, rows=rows, beta=beta_all[:, h:h + 1], gcum=gcum, glast=glast,
                    decay=jnp.exp(gmat - gmat.T),
                    q=x_ref[rows, h * DN_DK:(h + 1) * DN_DK],
                    k=x_ref[rows, DN_QK_W + h * DN_DK:DN_QK_W + (h + 1) * DN_DK],
                    v=x_ref[rows, 2 * DN_QK_W + h * DN_DV:2 * DN_QK_W + (h + 1) * DN_DV]))
        for u in units:
            u["kb"] = u["k"] * u["beta"]
            u["qk"] = _dot_nt(jnp.concatenate([u["q"], u["kb"]], axis=0).astype(BF16), u["k"].astype(BF16))
        for u in units:
            incl, strict = masks[u["d"]]
            u["attn"] = jnp.where(incl, u["qk"][:CHUNK] * u["decay"], 0.0).astype(BF16)
            n = -jnp.where(strict, u["qk"][CHUNK:] * u["decay"], 0.0)
            u["t"] = eye + n
            u["n"] = n.astype(BF16)
        for u in units:
            u["p"] = _dot(u["n"], u["n"])
        for _ in range(int(math.log2(CHUNK)) - 2):
            for u in units:
                u["both"] = _dot(jnp.concatenate([u["t"], u["p"]], axis=0).astype(BF16), u["p"].astype(BF16))
            for u in units:
                u["t"] = u["t"] + u["both"][:CHUNK]
                u["p"] = u["both"][CHUNK:]
        for u in units:
            u["tp"] = _dot(u["t"].astype(BF16), u["p"].astype(BF16))
        for u in units:
            tinv = (u["t"] + u["tp"]).astype(BF16)
            egc = jnp.exp(u["gcum"])
            rhs = jnp.concatenate([u["v"] * u["beta"], u["kb"] * egc], axis=1).astype(BF16)
            u["uw"] = _dot(tinv, rhs)
            u["q_dec"] = u["q"] * egc
            u["k_end"] = (u["k"] * jnp.exp(u["glast"] - u["gcum"])).astype(BF16)
        for u in units:
            u["state"] = st_ref[u["b"], u["d"], u["h"]]
            lhs = jnp.concatenate([u["uw"][:, DN_DV:], u["q_dec"]], axis=0).astype(BF16)
            u["wq"] = _dot(lhs, u["state"].astype(BF16))
        for u in units:
            u["v_new"] = (u["uw"][:, :DN_DV] - u["wq"][:CHUNK]).astype(BF16)
            u["o"] = u["wq"][CHUNK:] + _dot(u["attn"], u["v_new"])
        for u in units:
            st_ref[u["b"], u["d"], u["h"]] = (u["state"] * jnp.exp(u["glast"])
                                              + _dot_tn(u["k_end"], u["v_new"]))
        for b, d, _, _ in streams:
            sel = [u for u in units if u["b"] == b and u["d"] == d]
            o_refs[d][b, sel[0]["rows"], :] = jnp.concatenate([u["o"] for u in sel], axis=1)
        return carry

    lax.fori_loop(0, CHUNKS_PER_BLOCK, chunk_step, 0)


def _dn(x, ex, a_log, dt_bias, *, n_batch, n_lat_rows, seq):
    w = x.shape[1]
    tm = SEQ_BLOCK
    bps = seq // tm
    nlb = n_lat_rows // tm
    fwd = functools.partial(_seq_block, rev=False, n_lat_blocks=nlb, blocks_per_seq=bps)
    bwd = functools.partial(_seq_block, rev=True, n_lat_blocks=nlb, blocks_per_seq=bps)

    def specs(b, blk):
        return [pl.BlockSpec((tm, w), lambda s: (blk(b, s), 0)),
                pl.BlockSpec((tm, EX_WIDTH), lambda s: (blk(b, s), 0))]

    n_streams = 2 * n_batch
    vmem = (2 * n_streams * tm * (w + EX_WIDTH) * 4 + 2 * n_streams * tm * DN_V_W * 4
            + n_streams * DN_HEADS * DN_DK * DN_DV * 4 + n_streams * DN_HEADS * 40 * CHUNK * DN_DK * 4)
    out = jax.ShapeDtypeStruct((n_batch, seq + tm, DN_V_W), F32)
    return pl.pallas_call(
        functools.partial(_dn_kernel, n_batch=n_batch),
        out_shape=(out, out),
        grid=(bps + 1,),
        in_specs=[spec for b in range(n_batch) for blk in (fwd, bwd) for spec in specs(b, blk)] + [
            pl.BlockSpec((2, 1, DN_HEADS), lambda s: (0, 0, 0)),
            pl.BlockSpec((2, 1, DN_HEADS), lambda s: (0, 0, 0))],
        out_specs=_seq_out_specs(n_batch, tm, DN_V_W, bps),
        scratch_shapes=[pltpu.VMEM((n_batch, 2, DN_HEADS, DN_DK, DN_DV), F32)],
        compiler_params=_cparams(1, vmem),
        name="deltanet_scan",
    )(*([x, ex] * n_streams), a_log, dt_bias)


def _attn_kernel(q_ref, kc_ref, vtc_ref, kl_ref, vtl_ref, o_ref, m_ref, acc_ref,
                 *, with_ctx, tk):
    tq = q_ref.shape[0]
    ctx_chunk = (lambda: kc_ref[...], lambda: vtc_ref[...])
    lat_chunks = [(lambda c=c: kl_ref[c * tk:(c + 1) * tk, :], lambda c=c: vtl_ref[:, c * tk:(c + 1) * tk])
                  for c in range(kl_ref.shape[0] // tk)]

    def scores(chunk, g):
        return _dot_nt(chunk[0](), q_ref[:, g * ATT_HD:(g + 1) * ATT_HD])

    def accumulate(chunk, g, s, first):
        cols = slice(g * tq, (g + 1) * tq)
        m_new = jnp.max(s, axis=0, keepdims=True)
        if not first:
            m_old = m_ref[:, cols]
            m_new = jnp.maximum(m_old, m_new)
            alpha = jnp.exp2(m_old - m_new)
        p = jnp.exp2(s - m_new).astype(BF16)
        pv = _dot(chunk[1](), p)
        if not first:
            pv = alpha * acc_ref[:, cols] + pv
        m_ref[:, cols] = m_new
        acc_ref[:, cols] = pv

    def attend(chunks):
        units = [(ci, g) for ci in range(len(chunks)) for g in range(ATT_GROUP)]
        pending = [scores(chunks[ci], g) for ci, g in units[:ATT_LOOKAHEAD]]
        for i, (ci, g) in enumerate(units):
            if i + ATT_LOOKAHEAD < len(units):
                nci, ng = units[i + ATT_LOOKAHEAD]
                pending.append(scores(chunks[nci], ng))
            accumulate(chunks[ci], g, pending.pop(0), first=ci == 0)
        for g in range(ATT_GROUP):
            cols = slice(g * tq, (g + 1) * tq)
            o = acc_ref[:ATT_HD, cols] / acc_ref[ATT_HD:ATT_HD + 1, cols]
            o_ref[:, g * ATT_HD:(g + 1) * ATT_HD] = o.T

    def latent_query():
        attend([ctx_chunk] + lat_chunks)

    def context_query():
        attend([ctx_chunk])

    if with_ctx:
        is_ctx = pl.program_id(2) == 0
        pl.when(is_ctx)(context_query)
        pl.when(jnp.logical_not(is_ctx))(latent_query)
    else:
        latent_query()


def _attention(aq, ak, avt, *, n_batch, n_lat_rows, seq, ctx_len, with_ctx):
    n_rows = aq.shape[0]
    tq = SEQ_BLOCK
    assert ctx_len == tq
    bps = seq // tq
    nlb = n_lat_rows // tq
    tk = _pick(seq, (512, 256))
    n_q = bps + (1 if with_ctx else 0)
    q_cw = ATT_GROUP * ATT_HD

    def q_blk(b, qi):
        if with_ctx:
            return jnp.where(qi == 0, nlb + b, b * bps + qi - 1)
        return b * bps + qi

    vmem = (2 * tq * q_cw * (2 + 4) + 2 * 2 * (seq + ctx_len) * ATT_HD * 2
            + ATT_GROUP * tq * (2 * 8 + ATT_HD) * 4 + 6 * ATT_GROUP * tq * tk * 4)
    return pl.pallas_call(
        functools.partial(_attn_kernel, with_ctx=with_ctx, tk=tk),
        out_shape=jax.ShapeDtypeStruct((n_rows if with_ctx else n_lat_rows, ATT_Q_W), F32),
        grid=(n_batch, ATT_KV_HEADS, n_q),
        in_specs=[pl.BlockSpec((tq, q_cw), lambda b, h, qi: (q_blk(b, qi), h)),
                  pl.BlockSpec((ctx_len, ATT_HD), lambda b, h, qi: (nlb + b, h)),
                  pl.BlockSpec((VT_ROWS, ctx_len), lambda b, h, qi: (h, nlb + b)),
                  pl.BlockSpec((seq, ATT_HD), lambda b, h, qi: (b, h)),
                  pl.BlockSpec((VT_ROWS, seq), lambda b, h, qi: (h, b))],
        out_specs=pl.BlockSpec((tq, q_cw), lambda b, h, qi: (q_blk(b, qi), h)),
        scratch_shapes=[pltpu.VMEM((1, ATT_GROUP * tq), F32),
                        pltpu.VMEM((VT_ROWS, ATT_GROUP * tq), F32)],
        compiler_params=_cparams(3, vmem),
        name="gqa_attention",
    )(aq, ak, avt, ak, avt)


def _outproj_kernel(x_ref, mod_ref, gf_ref, gb_ref, gr_ref, df_ref, db_ref, dg_ref, at_ref,
                    gn_ref, dn_ref, w_ref, g_ref, b_ref, o_ref, *, mod_row, alpha):
    y = _dot(at_ref[...].astype(BF16), w_ref[GLA_V_W + DN_V_W:, :])
    row0 = 0
    for f_ref, b_ref_, gate_ref, norm_ref, heads, dv in ((gf_ref, gb_ref, gr_ref, gn_ref, GLA_HEADS, GLA_DV),
                                                         (df_ref, db_ref, dg_ref, dn_ref, DN_HEADS, DN_DV)):
        parts = []
        for h in range(heads):
            sl = slice(h * dv, (h + 1) * dv)
            gate = _silu(gate_ref[:, sl].astype(F32))
            o = _head_rmsnorm(f_ref[0, :, sl] + b_ref_[0, :, sl], norm_ref[...]) * gate
            parts.append(o.astype(BF16))
        y = y + _dot(jnp.concatenate(parts, axis=1), w_ref[row0:row0 + heads * dv, :])
        row0 += heads * dv
    gate = mod_ref[0, mod_row:mod_row + 1, :]
    o_ref[...] = _layernorm(alpha * x_ref[...] + gate * y, g_ref[...], b_ref[...])


def _outproj(x, mod, gla_f, gla_b, p, dn_f, dn_b, att, gla_norm_g, dn_norm_g, w_out, ln_g, ln_b,
             *, layer, mod_row, n_rows, n_batch, n_lat_rows, seq, alpha):
    d = x.shape[1]
    tm = SEQ_BLOCK
    idx = functools.partial(_mod_block_index, tm=tm, n_lat_rows=n_lat_rows, seq=seq, n_batch=n_batch)
    row = lambda w, cb: pl.BlockSpec((tm, w), lambda i: (i, cb))
    vec = lambda w: pl.BlockSpec((1, w), lambda i: (0, 0))
    bps = seq // tm
    nlb = n_lat_rows // tm
    scan = lambda w: pl.BlockSpec((1, tm, w), lambda i: (jnp.where(i < nlb, i // bps, i - nlb),
                                                         jnp.where(i < nlb, 1 + i % bps, 0), 0))
    vmem = (2 * tm * (2 * d + 6 * GLA_V_W + ATT_Q_W) * 4 + 2 * MIX_WIDTH * d * 2
            + tm * MIX_WIDTH * 2 + 6 * tm * d * 4)
    return pl.pallas_call(
        functools.partial(_outproj_kernel, mod_row=mod_row, alpha=alpha),
        out_shape=jax.ShapeDtypeStruct((n_rows, d), F32),
        grid=(n_rows // tm,),
        in_specs=[row(d, 0),
                  pl.BlockSpec((None, 1, N_MOD, d), lambda i: (layer, idx(i), 0, 0)),
                  scan(GLA_V_W), scan(GLA_V_W), row(GLA_V_W, OFF_GLA_R // GLA_V_W),
                  scan(DN_V_W), scan(DN_V_W), row(DN_V_W, OFF_DN_GATE // DN_V_W),
                  row(ATT_Q_W, 0),
                  vec(GLA_DV), vec(DN_DV),
                  pl.BlockSpec((None, MIX_WIDTH, d), lambda i: (layer, 0, 0)),
                  vec(d), vec(d)],
        out_specs=row(d, 0),
        compiler_params=_cparams(1, vmem),
        name="mixer_out_proj",
    )(x, mod, gla_f, gla_b, p, dn_f, dn_b, p, att, gla_norm_g, dn_norm_g, w_out, ln_g, ln_b)


def _rope_tables(seq):
    t = jnp.arange(seq)
    pr = (t // GRID_W).astype(F32)
    pc = (t % GRID_W).astype(F32)
    axis_dim = ATT_HD // 2
    inv = ROPE_THETA ** (-jnp.arange(0, axis_dim, 2, dtype=F32) / axis_dim)
    ar = pr[:, None] * inv
    ac = pc[:, None] * inv
    cos = jnp.concatenate([jnp.cos(ar), jnp.cos(ar), jnp.cos(ac), jnp.cos(ac)], axis=1)
    sin = jnp.concatenate([-jnp.sin(ar), jnp.sin(ar), -jnp.sin(ac), jnp.sin(ac)], axis=1)
    return cos, sin


def kernel(x, c, ctx, c_ctx, w_ada, b_ada, ln_g, ln_b, w_ffn_gate, w_ffn_up, w_ffn_down, w_in,
           gla_wa1, gla_wa2, gla_ba, gla_norm_g, dn_conv, dn_wab, dn_a_log, dn_dt_bias, dn_norm_g,
           q_norm_g, k_norm_g, w_out):
    n_batch, seq, d = x.shape
    ctx_len = ctx.shape[1]
    depth = w_ada.shape[0]
    alpha = (2 * depth) ** 0.25
    n_lat = n_batch * seq
    n_all = n_lat + n_batch * ctx_len
    assert n_batch + 1 <= 8 and seq % SEQ_BLOCK == 0 and ctx_len == SEQ_BLOCK

    cond = jnp.zeros((8, d), F32).at[:n_batch].set(c).at[n_batch].set(c_ctx)
    mod_all = _ada_table(cond, w_ada, b_ada).reshape(depth, 8, N_MOD, d)
    rope_cos, rope_sin = _rope_tables(seq)
    xs = x.reshape(n_lat, d)
    x_ctx = ctx.reshape(n_batch * ctx_len, d)
    ffn_w = (w_ffn_gate.astype(BF16), w_ffn_up.astype(BF16), w_ffn_down.astype(BF16))
    w_in_b = w_in.astype(BF16)
    w_out_b = w_out.astype(BF16)

    for layer in range(depth):
        last = layer == depth - 1
        mod = mod_all
        lg = ln_g[layer].reshape(3, 1, d)
        lb = ln_b[layer].reshape(3, 1, d)
        w_ex = jnp.concatenate([gla_wa1[layer, 0], gla_wa1[layer, 1], dn_wab[layer, 0], dn_wab[layer, 1]], axis=1)
        w_ex = jnp.pad(w_ex, ((0, 0), (0, EX_WIDTH - w_ex.shape[1]))).astype(BF16)
        common = dict(n_lat_rows=n_lat, seq=seq)

        xs = _ffn(xs, mod, *ffn_w, lg[0], lb[0], w_index=(layer, 0), mod_row=0, n_rows=n_all, alpha=alpha,
                  x_ctx=x_ctx if layer == 0 else None, n_batch=n_batch, **common)
        p, ex = _inproj(xs, mod, w_in_b, w_ex, layer=layer, mod_row=3, n_batch=n_batch, **common)
        dn_x, aq, ak, avt = _prep(p, dn_conv[layer], q_norm_g[layer].reshape(1, ATT_HD),
                              k_norm_g[layer].reshape(1, ATT_HD), rope_cos, rope_sin, **common)
        gla_f, gla_b = _gla(p, ex, gla_wa2[layer], gla_ba[layer].reshape(2, 1, GLA_QK_W),
                            n_batch=n_batch, **common)
        dn_f, dn_b = _dn(dn_x, ex, dn_a_log[layer].reshape(2, 1, DN_HEADS),
                         dn_dt_bias[layer].reshape(2, 1, DN_HEADS), n_batch=n_batch, **common)
        att = _attention(aq, ak, avt, n_batch=n_batch, ctx_len=ctx_len, with_ctx=not last, **common)
        n_out = n_lat if last else n_all
        xs = _outproj(xs, mod, gla_f, gla_b, p, dn_f, dn_b, att, gla_norm_g[layer].reshape(1, GLA_DV),
                      dn_norm_g[layer].reshape(1, DN_DV), w_out_b, lg[1], lb[1],
                      layer=layer, mod_row=5, n_rows=n_out, alpha=alpha, n_batch=n_batch, **common)
        xs = _ffn(xs, mod, *ffn_w, lg[2], lb[2], w_index=(layer, 1), mod_row=6, n_rows=n_out, alpha=alpha,
                  n_batch=n_batch, **common)
    return xs.reshape(n_batch, seq, d)
```

```python
import functools
import math

import jax
import jax.numpy as jnp
from jax import lax
from jax.experimental import pallas as pl
from jax.experimental.pallas import tpu as pltpu

F32 = jnp.float32
BF16 = jnp.bfloat16

N_MOD = 9
GRID_W = 64
MACARON_WEIGHT = 0.5
GLA_HEADS, GLA_DK, GLA_DV, GLA_RANK, GLA_TAU = 4, 64, 128, 16, 16.0
DN_HEADS, DN_DK, DN_DV, DN_CONV = 4, 128, 128, 3
ATT_HEADS, ATT_KV_HEADS, ATT_HD = 8, 2, 128
ATT_GROUP = ATT_HEADS // ATT_KV_HEADS
VT_ROWS = ATT_HD + 16
ATT_LOOKAHEAD = 3
CHUNK = 64
ROPE_THETA = 10000.0
NORM_EPS = 1e-6
LOG2_E = math.log2(math.e)
GLA_QK_W, GLA_V_W = GLA_HEADS * GLA_DK, GLA_HEADS * GLA_DV
DN_QK_W, DN_V_W = DN_HEADS * DN_DK, DN_HEADS * DN_DV
ATT_Q_W, ATT_KV_W = ATT_HEADS * ATT_HD, ATT_KV_HEADS * ATT_HD
IN_WIDTH = 2 * GLA_QK_W + 2 * GLA_V_W + 2 * DN_QK_W + 2 * DN_V_W + ATT_Q_W + 2 * ATT_KV_W
MIX_WIDTH = GLA_V_W + DN_V_W + ATT_Q_W
OFF_GLA_V, OFF_GLA_R = 2 * GLA_QK_W, 2 * GLA_QK_W + GLA_V_W
OFF_DN = OFF_GLA_R + GLA_V_W
OFF_DN_GATE = OFF_DN + 2 * DN_QK_W + DN_V_W
OFF_ATT = OFF_DN_GATE + DN_V_W
EX_WIDTH = 128
EX_DN = 2 * GLA_RANK

V7X_VMEM_BYTES = 64 * 1024 * 1024
LANES = 128
SEQ_BLOCK = 256
CHUNKS_PER_BLOCK = SEQ_BLOCK // CHUNK
HALO_ROWS = 16


def _cparams(n_axes, vmem_bytes):
    limit = int(min(max(vmem_bytes, 16 * 1024 * 1024), V7X_VMEM_BYTES - 8 * 1024 * 1024))
    return pltpu.CompilerParams(dimension_semantics=("arbitrary",) * n_axes,
                                vmem_limit_bytes=limit)


def _dot(a, b):
    return jnp.dot(a, b, preferred_element_type=F32)


def _dot_nt(a, b):
    return lax.dot_general(a, b, (((1,), (1,)), ((), ())), preferred_element_type=F32)


def _dot_tn(a, b):
    return lax.dot_general(a, b, (((0,), (0,)), ((), ())), preferred_element_type=F32)


def _sigmoid(x):
    return 1.0 / (1.0 + jnp.exp(-x))


def _silu(x):
    return x * _sigmoid(x)


def _softplus(x):
    return jnp.maximum(x, 0.0) + jnp.log(1.0 + jnp.exp(-jnp.abs(x)))


def _layernorm(z, g, b):
    mu = jnp.mean(z, axis=-1, keepdims=True)
    zc = z - mu
    var = jnp.mean(zc * zc, axis=-1, keepdims=True)
    return zc * lax.rsqrt(var + NORM_EPS) * g + b


def _head_rmsnorm(x, g):
    return x * lax.rsqrt(jnp.mean(x * x, axis=-1, keepdims=True) + NORM_EPS) * g


def _cumsum_f32(tri, g):
    g1 = g.astype(BF16)
    r1 = g - g1.astype(F32)
    g2 = r1.astype(BF16)
    g3 = (r1 - g2.astype(F32)).astype(BF16)
    return _dot(tri, g1) + _dot(tri, g2) + _dot(tri, g3)


def _pick(n, candidates):
    for c in candidates:
        if n % c == 0:
            return c
    raise ValueError(f"no block size in {candidates} divides {n}")


def _mod_block_index(i, tm, n_lat_rows, seq, n_batch):
    assert seq % tm == 0 and n_lat_rows % tm == 0
    r = i * tm
    return jnp.where(r < n_lat_rows, r // seq, n_batch)


def _ada_kernel(c_ref, w_ref, b_ref, o_ref):
    s = _silu(c_ref[...]).astype(BF16)
    o_ref[0] = _dot(s, w_ref[0].astype(BF16)) + b_ref[0]


def _ada_table(cond, w_ada, b_ada):
    n_layer, d, n = w_ada.shape
    tn = _pick(n, (1024, 512, 256, 128))
    vmem = 2 * d * tn * 4 + d * tn * 2 + 4 * 8 * (d + 2 * tn) * 4
    return pl.pallas_call(
        _ada_kernel,
        out_shape=jax.ShapeDtypeStruct((n_layer, 8, n), F32),
        grid=(n_layer, n // tn),
        in_specs=[pl.BlockSpec((8, d), lambda l, j: (0, 0)),
                  pl.BlockSpec((1, d, tn), lambda l, j: (l, 0, j)),
                  pl.BlockSpec((1, 1, tn), lambda l, j: (l, 0, j))],
        out_specs=pl.BlockSpec((1, 8, tn), lambda l, j: (l, 0, j)),
        compiler_params=_cparams(2, vmem),
        name="ada_table",
    )(cond, w_ada, b_ada.reshape(n_layer, 1, n))


def _ffn_kernel(*refs, mod_row, alpha, n_lat_blocks, split_x):
    if split_x:
        x_ref, xc_ref, *refs = refs
    else:
        x_ref, *refs = refs
    mod_ref, wg_ref, wu_ref, wd_ref, g_ref, b_ref, o_ref, h_ref = refs
    acc_ref = o_ref
    j = pl.program_id(1)

    def load_x():
        if not split_x:
            return x_ref[...]
        return jnp.where(pl.program_id(0) < n_lat_blocks, x_ref[...], xc_ref[...])

    @pl.when(j == 0)
    def _():
        shift = mod_ref[0, mod_row:mod_row + 1, :]
        scale = mod_ref[0, mod_row + 1:mod_row + 2, :]
        h_ref[...] = (load_x() * (1.0 + scale) + shift).astype(BF16)
        acc_ref[...] = jnp.zeros(acc_ref.shape, F32)

    h = h_ref[...]
    a = _dot(h, wg_ref[...])
    u = _dot(h, wu_ref[...])
    acc_ref[...] += _dot((_silu(a) * u).astype(BF16), wd_ref[...])

    @pl.when(j == pl.num_programs(1) - 1)
    def _():
        gate = mod_ref[0, mod_row + 2:mod_row + 3, :]
        z = alpha * load_x() + (MACARON_WEIGHT * gate) * acc_ref[...]
        o_ref[...] = _layernorm(z, g_ref[...], b_ref[...])


def _ffn(x, mod, wg, wu, wd, ln_g, ln_b, *, w_index, mod_row, n_rows, n_batch, n_lat_rows, seq,
         alpha, x_ctx=None):
    d = x.shape[1]
    f = wg.shape[-1]
    split_x = x_ctx is not None
    tm = _pick(math.gcd(n_rows, seq), (512, 256) if split_x else (1024, 512, 256))
    tf = _pick(f, (512, 256, 128))
    nlb = n_lat_rows // tm
    vmem = ((5 if split_x else 3) * tm * d * 4 + tm * d * 2 + 2 * 3 * d * tf * 2
            + 8 * tm * tf * 4 + 2 * 16 * d * 4)
    idx = functools.partial(_mod_block_index, tm=tm, n_lat_rows=n_lat_rows, seq=seq, n_batch=n_batch)
    if split_x:
        x_specs = [pl.BlockSpec((tm, d), lambda i, j: (jnp.minimum(i, nlb - 1), 0)),
                   pl.BlockSpec((tm, d), lambda i, j: (jnp.maximum(i - nlb, 0), 0))]
        xs = (x, x_ctx)
    else:
        x_specs = [pl.BlockSpec((tm, d), lambda i, j: (i, 0))]
        xs = (x,)
    return pl.pallas_call(
        functools.partial(_ffn_kernel, mod_row=mod_row, alpha=alpha, n_lat_blocks=nlb, split_x=split_x),
        out_shape=jax.ShapeDtypeStruct((n_rows, d), F32),
        grid=(n_rows // tm, f // tf),
        in_specs=x_specs + [
            pl.BlockSpec((None, 1, N_MOD, d), lambda i, j: (w_index[0], idx(i), 0, 0)),
            pl.BlockSpec((None, None, d, tf), lambda i, j: (*w_index, 0, j)),
            pl.BlockSpec((None, None, d, tf), lambda i, j: (*w_index, 0, j)),
            pl.BlockSpec((None, None, tf, d), lambda i, j: (*w_index, j, 0)),
            pl.BlockSpec((1, d), lambda i, j: (0, 0)),
            pl.BlockSpec((1, d), lambda i, j: (0, 0))],
        out_specs=pl.BlockSpec((tm, d), lambda i, j: (i, 0), pipeline_mode=pl.Buffered(1)),
        scratch_shapes=[pltpu.VMEM((tm, d), BF16)],
        compiler_params=_cparams(2, vmem),
        name="ffn_sublayer",
    )(*xs, mod, wg, wu, wd, ln_g, ln_b)


def _inproj_kernel(x_ref, mod_ref, w_ref, wex_ref, p_ref, ex_ref, h_ref, *, mod_row):
    j = pl.program_id(1)

    @pl.when(j == 0)
    def _():
        shift = mod_ref[0, mod_row:mod_row + 1, :]
        scale = mod_ref[0, mod_row + 1:mod_row + 2, :]
        h = (x_ref[...] * (1.0 + scale) + shift).astype(BF16)
        h_ref[...] = h
        ex_ref[...] = _dot(h, wex_ref[...])

    p_ref[...] = _dot(h_ref[...], w_ref[...]).astype(p_ref.dtype)


def _inproj(x, mod, w_in, w_ex, *, layer, mod_row, n_batch, n_lat_rows, seq):
    n_rows, d = x.shape
    n = w_in.shape[-1]
    tm = _pick(math.gcd(n_rows, seq), (1024, 512, 256))
    tn = _pick(n, (1280, 512, 256, 128))
    vmem = (2 * tm * d * 4 + tm * d * 2 + 2 * d * tn * 2 + 2 * d * EX_WIDTH * 2
            + 2 * tm * tn * 2 + 2 * tm * tn * 4 + 2 * tm * EX_WIDTH * 4 + 2 * 16 * d * 4 + 2 * tm * d * 4)
    idx = functools.partial(_mod_block_index, tm=tm, n_lat_rows=n_lat_rows, seq=seq, n_batch=n_batch)
    return pl.pallas_call(
        functools.partial(_inproj_kernel, mod_row=mod_row),
        out_shape=(jax.ShapeDtypeStruct((n_rows, n), BF16),
                   jax.ShapeDtypeStruct((n_rows, EX_WIDTH), F32)),
        grid=(n_rows // tm, n // tn),
        in_specs=[pl.BlockSpec((tm, d), lambda i, j: (i, 0)),
                  pl.BlockSpec((None, 1, N_MOD, d), lambda i, j: (layer, idx(i), 0, 0)),
                  pl.BlockSpec((None, d, tn), lambda i, j: (layer, 0, j)),
                  pl.BlockSpec((d, EX_WIDTH), lambda i, j: (0, 0))],
        out_specs=(pl.BlockSpec((tm, tn), lambda i, j: (i, j)),
                   pl.BlockSpec((tm, EX_WIDTH), lambda i, j: (i, 0))),
        scratch_shapes=[pltpu.VMEM((tm, d), BF16)],
        compiler_params=_cparams(2, vmem),
        name="mixer_in_proj",
    )(x, mod, w_in, w_ex)


def _prep_kernel(dn_ref, prev_ref, next_ref, conv_ref, aq0_ref, aq1_ref, akv_ref, qg_ref, kg_ref,
                 cos_ref, sin_ref, dn_o, aq_o, ak_o, avt_o, *, n_lat_blocks, blocks_per_seq):
    i = pl.program_id(0)
    is_lat = i < n_lat_blocks
    pos = i % blocks_per_seq
    has_prev = jnp.logical_and(is_lat, pos != 0)
    has_next = jnp.logical_and(is_lat, pos != blocks_per_seq - 1)

    z = dn_ref[...].astype(F32)
    rows = z.shape[0]
    ridx = lax.broadcasted_iota(jnp.int32, z.shape, 0)
    halo_p = jnp.where(has_prev, prev_ref[HALO_ROWS - 1:HALO_ROWS, :].astype(F32), 0.0)
    halo_n = jnp.where(has_next, next_ref[0:1, :].astype(F32), 0.0)
    z_prev = jnp.where(ridx == 0, halo_p, pltpu.roll(z, 1, 0))
    z_next = jnp.where(ridx == rows - 1, halo_n, pltpu.roll(z, rows - 1, 0))
    y = _silu(z_prev * conv_ref[0:1, :] + z * conv_ref[1:2, :] + z_next * conv_ref[2:3, :])
    for hh in range(2 * DN_HEADS):
        sl = slice(hh * DN_DK, (hh + 1) * DN_DK)
        yh = y[:, sl]
        yn = yh * lax.rsqrt(jnp.sum(yh * yh, axis=-1, keepdims=True) + NORM_EPS)
        if hh < DN_HEADS:
            yn = yn * (DN_DK ** -0.5)
        dn_o[:, sl] = yn
    dn_o[:, 2 * DN_QK_W:] = y[:, 2 * DN_QK_W:]

    lane = lax.broadcasted_iota(jnp.int32, (rows, ATT_HD), 1)
    first_half_of_pair = (lane // (ATT_HD // 4)) % 2 == 0
    cos = cos_ref[...]
    sin = sin_ref[...]

    def rope(xh):
        swapped = jnp.where(first_half_of_pair,
                            pltpu.roll(xh, ATT_HD - ATT_HD // 4, 1), pltpu.roll(xh, ATT_HD // 4, 1))
        return jnp.where(is_lat, xh * cos + swapped * sin, xh)

    for hh in range(ATT_HEADS):
        src = aq0_ref if hh < ATT_HEADS // 2 else aq1_ref
        off = (hh % (ATT_HEADS // 2)) * ATT_HD
        qh = rope(_head_rmsnorm(src[:, off:off + ATT_HD].astype(F32), qg_ref[...])) * (LOG2_E * ATT_HD ** -0.5)
        aq_o[:, hh * ATT_HD:(hh + 1) * ATT_HD] = qh.astype(BF16)
    for hh in range(ATT_KV_HEADS):
        sl = slice(hh * ATT_HD, (hh + 1) * ATT_HD)
        ak_o[:, sl] = rope(_head_rmsnorm(akv_ref[:, sl].astype(F32), kg_ref[...])).astype(BF16)
        vt = akv_ref[:, ATT_KV_W + hh * ATT_HD:ATT_KV_W + (hh + 1) * ATT_HD].astype(F32).T.astype(BF16)
        avt_o[hh * VT_ROWS:hh * VT_ROWS + ATT_HD, :] = vt
        avt_o[hh * VT_ROWS + ATT_HD:(hh + 1) * VT_ROWS, :] = jnp.ones((VT_ROWS - ATT_HD, rows), BF16)


def _prep(p, conv_w, q_norm_g, k_norm_g, rope_cos, rope_sin, *, n_lat_rows, seq):
    n_rows = p.shape[0]
    tm = SEQ_BLOCK
    n_blocks = n_rows // tm
    n_lat_blocks = n_lat_rows // tm
    bps = seq // tm
    dn_w = 2 * DN_QK_W + DN_V_W
    dn_cb = OFF_DN // dn_w
    assert OFF_DN % dn_w == 0 and OFF_ATT % 512 == 0
    att_cb = OFF_ATT // 512
    halo_per_block = tm // HALO_ROWS
    n_halo = n_rows // HALO_ROWS
    vmem = (2 * 2 * tm * dn_w * 4 + 2 * 3 * tm * 512 * 4 + 2 * tm * 1536 * 2
            + 10 * tm * dn_w * 4 + 4 * tm * LANES * 4)
    return pl.pallas_call(
        functools.partial(_prep_kernel, n_lat_blocks=n_lat_blocks, blocks_per_seq=bps),
        out_shape=(jax.ShapeDtypeStruct((n_rows, dn_w), F32),
                   jax.ShapeDtypeStruct((n_rows, ATT_Q_W), BF16),
                   jax.ShapeDtypeStruct((n_rows, ATT_KV_W), BF16),
                   jax.ShapeDtypeStruct((ATT_KV_HEADS * VT_ROWS, n_rows), BF16)),
        grid=(n_blocks,),
        in_specs=[pl.BlockSpec((tm, dn_w), lambda i: (i, dn_cb)),
                  pl.BlockSpec((HALO_ROWS, dn_w), lambda i: (jnp.maximum(i * halo_per_block - 1, 0), dn_cb)),
                  pl.BlockSpec((HALO_ROWS, dn_w),
                               lambda i: (jnp.minimum((i + 1) * halo_per_block, n_halo - 1), dn_cb)),
                  pl.BlockSpec((DN_CONV, dn_w), lambda i: (0, 0)),
                  pl.BlockSpec((tm, 512), lambda i: (i, att_cb)),
                  pl.BlockSpec((tm, 512), lambda i: (i, att_cb + 1)),
                  pl.BlockSpec((tm, 512), lambda i: (i, att_cb + 2)),
                  pl.BlockSpec((1, ATT_HD), lambda i: (0, 0)),
                  pl.BlockSpec((1, ATT_HD), lambda i: (0, 0)),
                  pl.BlockSpec((tm, ATT_HD), lambda i: (jnp.where(i < n_lat_blocks, i % bps, 0), 0)),
                  pl.BlockSpec((tm, ATT_HD), lambda i: (jnp.where(i < n_lat_blocks, i % bps, 0), 0))],
        out_specs=(pl.BlockSpec((tm, dn_w), lambda i: (i, 0)),
                   pl.BlockSpec((tm, ATT_Q_W), lambda i: (i, 0)),
                   pl.BlockSpec((tm, ATT_KV_W), lambda i: (i, 0)),
                   pl.BlockSpec((ATT_KV_HEADS * VT_ROWS, tm), lambda i: (0, i))),
        compiler_params=_cparams(1, vmem),
        name="mixer_prep",
    )(p, p, p, conv_w, p, p, p, q_norm_g, k_norm_g, rope_cos, rope_sin)


def _seq_block(b, s, *, rev, n_lat_blocks, blocks_per_seq):
    j = (blocks_per_seq - s) if rev else (s - 1)
    return jnp.where(s == 0, n_lat_blocks + b, b * blocks_per_seq + j)


def _tri_masks(rev):
    row = lax.broadcasted_iota(jnp.int32, (CHUNK, CHUNK), 0)
    col = lax.broadcasted_iota(jnp.int32, (CHUNK, CHUNK), 1)
    incl = (row <= col) if rev else (row >= col)
    strict = (row < col) if rev else (row > col)
    return incl, strict


def _gla_kernel(*refs, n_batch):
    n_in = n_batch * 2 * 4
    wa2_ref, ba_ref, of_ref, ob_ref, st_ref = refs[n_in:]
    o_refs = (of_ref, ob_ref)
    streams = [(b, d) + tuple(refs[(b * 2 + d) * 4:(b * 2 + d + 1) * 4])
               for b in range(n_batch) for d in range(2)]

    @pl.when(pl.program_id(0) == 0)
    def _():
        st_ref[...] = jnp.zeros(st_ref.shape, F32)

    masks = [_tri_masks(rev)[0] for rev in (False, True)]
    tris = [jnp.where(m, 1.0, 0.0).astype(BF16) for m in masks]
    units = [(b, d, h) for b in range(n_batch) for d in range(2) for h in range(GLA_HEADS)]
    hs = lambda h: slice(h * GLA_DK, (h + 1) * GLA_DK)
    vs = lambda h: slice(h * GLA_DV, (h + 1) * GLA_DV)

    def chunk_step(c, carry):
        prep = {}
        for b, d, q_ref, k_ref, v_ref, e_ref in streams:
            rev = d == 1
            cc = (CHUNKS_PER_BLOCK - 1 - c) if rev else c
            rows = pl.ds(pl.multiple_of(cc * CHUNK, CHUNK), CHUNK)
            low_rank = e_ref[rows, d * GLA_RANK:(d + 1) * GLA_RANK].astype(BF16)
            logit = _dot(low_rank, wa2_ref[d].astype(BF16)) + ba_ref[d]
            log_a = -_softplus(-logit) / GLA_TAU
            bcum = _cumsum_f32(tris[d], log_a)
            blast = bcum[0:1] if rev else bcum[CHUNK - 1:CHUNK]
            kc = k_ref[rows, :].astype(F32)
            prep[b, d] = dict(
                rows=rows,
                q_in=(q_ref[rows, :].astype(F32) * (GLA_DK ** -0.5) * jnp.exp(bcum)).astype(BF16),
                k_in=(kc * jnp.exp(-bcum)).astype(BF16),
                k_end=(kc * jnp.exp(blast - bcum)).astype(BF16),
                decay=jnp.exp(blast),
                v=v_ref[rows, :].astype(BF16),
                st=st_ref[b, d])
        attn = {(b, d, h): jnp.where(masks[d], _dot_nt(prep[b, d]["q_in"][:, hs(h)],
                                                        prep[b, d]["k_in"][:, hs(h)]), 0.0).astype(BF16)
                for b, d, h in units}
        inter = {(b, d, h): _dot_nt(prep[b, d]["q_in"][:, hs(h)], prep[b, d]["st"][:, hs(h)].astype(BF16))
                 for b, d, h in units}
        intra = {(b, d, h): _dot(attn[b, d, h], prep[b, d]["v"][:, vs(h)]) for b, d, h in units}
        upd = {(b, d, h): _dot_tn(prep[b, d]["v"][:, vs(h)], prep[b, d]["k_end"][:, hs(h)])
               for b, d, h in units}
        for (b, d), pr in prep.items():
            o_refs[d][b, pr["rows"], :] = jnp.concatenate(
                [intra[b, d, h] + inter[b, d, h] for h in range(GLA_HEADS)], axis=1)
            st_ref[b, d] = pr["st"] * pr["decay"] + jnp.concatenate(
                [upd[b, d, h] for h in range(GLA_HEADS)], axis=1)
        return carry

    lax.fori_loop(0, CHUNKS_PER_BLOCK, chunk_step, 0)


def _seq_out_specs(n_batch, tm, width, bps):
    return (pl.BlockSpec((n_batch, tm, width), lambda s: (0, s, 0)),
            pl.BlockSpec((n_batch, tm, width), lambda s: (0, jnp.where(s == 0, 0, bps + 1 - s), 0)))


def _gla(p, ex, wa2, ba, *, n_batch, n_lat_rows, seq):
    tm = SEQ_BLOCK
    bps = seq // tm
    nlb = n_lat_rows // tm
    fwd = functools.partial(_seq_block, rev=False, n_lat_blocks=nlb, blocks_per_seq=bps)
    bwd = functools.partial(_seq_block, rev=True, n_lat_blocks=nlb, blocks_per_seq=bps)
    v_cb = OFF_GLA_V // GLA_V_W

    def specs(b, blk):
        return [pl.BlockSpec((tm, GLA_QK_W), lambda s: (blk(b, s), 0)),
                pl.BlockSpec((tm, GLA_QK_W), lambda s: (blk(b, s), 1)),
                pl.BlockSpec((tm, GLA_V_W), lambda s: (blk(b, s), v_cb)),
                pl.BlockSpec((tm, EX_WIDTH), lambda s: (blk(b, s), 0))]

    n_streams = 2 * n_batch
    vmem = (2 * n_streams * tm * (2 * GLA_QK_W + GLA_V_W + EX_WIDTH) * 4 + 2 * n_streams * tm * GLA_V_W * 4
            + n_streams * GLA_DV * GLA_QK_W * 4 + n_streams * 40 * CHUNK * GLA_QK_W * 4)
    out = jax.ShapeDtypeStruct((n_batch, seq + tm, GLA_V_W), F32)
    return pl.pallas_call(
        functools.partial(_gla_kernel, n_batch=n_batch),
        out_shape=(out, out),
        grid=(bps + 1,),
        in_specs=[spec for b in range(n_batch) for blk in (fwd, bwd) for spec in specs(b, blk)] + [
            pl.BlockSpec((2, GLA_RANK, GLA_QK_W), lambda s: (0, 0, 0)),
            pl.BlockSpec((2, 1, GLA_QK_W), lambda s: (0, 0, 0))],
        out_specs=_seq_out_specs(n_batch, tm, GLA_V_W, bps),
        scratch_shapes=[pltpu.VMEM((n_batch, 2, GLA_DV, GLA_QK_W), F32)],
        compiler_params=_cparams(1, vmem),
        name="gla_scan",
    )(*([p, p, p, ex] * n_streams), wa2, ba)


def _dn_kernel(*refs, n_batch):
    n_in = n_batch * 2 * 2
    alog_ref, dtb_ref, of_ref, ob_ref, st_ref = refs[n_in:]
    o_refs = (of_ref, ob_ref)
    streams = [(b, d) + tuple(refs[(b * 2 + d) * 2:(b * 2 + d + 1) * 2])
               for b in range(n_batch) for d in range(2)]

    @pl.when(pl.program_id(0) == 0)
    def _():
        st_ref[...] = jnp.zeros(st_ref.shape, F32)

    masks = [_tri_masks(rev) for rev in (False, True)]
    tris = [jnp.where(m[0], 1.0, 0.0).astype(BF16) for m in masks]
    row = lax.broadcasted_iota(jnp.int32, (CHUNK, CHUNK), 0)
    col = lax.broadcasted_iota(jnp.int32, (CHUNK, CHUNK), 1)
    eye = jnp.where(row == col, 1.0, 0.0)

    def chunk_step(c, carry):
        units = []
        for b, d, x_ref, e_ref in streams:
            rev = d == 1
            cc = (CHUNKS_PER_BLOCK - 1 - c) if rev else c
            rows = pl.ds(pl.multiple_of(cc * CHUNK, CHUNK), CHUNK)
            lo = EX_DN + 2 * DN_HEADS * d
            a_in = e_ref[rows, lo:lo + DN_HEADS]
            b_in = e_ref[rows, lo + DN_HEADS:lo + 2 * DN_HEADS]
            g = -jnp.exp(alog_ref[d]) * _softplus(a_in + dtb_ref[d])
            beta_all = _sigmoid(b_in)
            gcum_all = _cumsum_f32(tris[d], g)
            for h in range(DN_HEADS):
                gcum = gcum_all[:, h:h + 1]
                glast = gcum[0:1] if rev else gcum[CHUNK - 1:CHUNK]
                gmat = jnp.broadcast_to(gcum, (CHUNK, CHUNK))
                units.append(dict(
                    b=b, d=d, h=h, rows=rows, beta=beta_all[:, h:h + 1], gcum=gcum, glast=glast,
                    decay=jnp.exp(gmat - gmat.T),
                    q=x_ref[rows, h * DN_DK:(h + 1) * DN_DK],
                    k=x_ref[rows, DN_QK_W + h * DN_DK:DN_QK_W + (h + 1) * DN_DK],
                    v=x_ref[rows, 2 * DN_QK_W + h * DN_DV:2 * DN_QK_W + (h + 1) * DN_DV]))
        for u in units:
            u["kb"] = u["k"] * u["beta"]
            u["qk"] = _dot_nt(jnp.concatenate([u["q"], u["kb"]], axis=0).astype(BF16), u["k"].astype(BF16))
        for u in units:
            incl, strict = masks[u["d"]]
            u["attn"] = jnp.where(incl, u["qk"][:CHUNK] * u["decay"], 0.0).astype(BF16)
            n = -jnp.where(strict, u["qk"][CHUNK:] * u["decay"], 0.0)
            u["t"] = eye + n
            u["n"] = n.astype(BF16)
        for u in units:
            u["p"] = _dot(u["n"], u["n"])
        for _ in range(int(math.log2(CHUNK)) - 2):
            for u in units:
                u["both"] = _dot(jnp.concatenate([u["t"], u["p"]], axis=0).astype(BF16), u["p"].astype(BF16))
            for u in units:
                u["t"] = u["t"] + u["both"][:CHUNK]
                u["p"] = u["both"][CHUNK:]
        for u in units:
            u["tp"] = _dot(u["t"].astype(BF16), u["p"].astype(BF16))
        for u in units:
            tinv = (u["t"] + u["tp"]).astype(BF16)
            egc = jnp.exp(u["gcum"])
            rhs = jnp.concatenate([u["v"] * u["beta"], u["kb"] * egc], axis=1).astype(BF16)
            u["uw"] = _dot(tinv, rhs)
            u["q_dec"] = u["q"] * egc
            u["k_end"] = (u["k"] * jnp.exp(u["glast"] - u["gcum"])).astype(BF16)
        for u in units:
            u["state"] = st_ref[u["b"], u["d"], u["h"]]
            lhs = jnp.concatenate([u["uw"][:, DN_DV:], u["q_dec"]], axis=0).astype(BF16)
            u["wq"] = _dot(lhs, u["state"].astype(BF16))
        for u in units:
            u["v_new"] = (u["uw"][:, :DN_DV] - u["wq"][:CHUNK]).astype(BF16)
            u["o"] = u["wq"][CHUNK:] + _dot(u["attn"], u["v_new"])
        for u in units:
            st_ref[u["b"], u["d"], u["h"]] = (u["state"] * jnp.exp(u["glast"])
                                              + _dot_tn(u["k_end"], u["v_new"]))
        for b, d, _, _ in streams:
            sel = [u for u in units if u["b"] == b and u["d"] == d]
            o_refs[d][b, sel[0]["rows"], :] = jnp.concatenate([u["o"] for u in sel], axis=1)
        return carry

    lax.fori_loop(0, CHUNKS_PER_BLOCK, chunk_step, 0)


def _dn(x, ex, a_log, dt_bias, *, n_batch, n_lat_rows, seq):
    w = x.shape[1]
    tm = SEQ_BLOCK
    bps = seq // tm
    nlb = n_lat_rows // tm
    fwd = functools.partial(_seq_block, rev=False, n_lat_blocks=nlb, blocks_per_seq=bps)
    bwd = functools.partial(_seq_block, rev=True, n_lat_blocks=nlb, blocks_per_seq=bps)

    def specs(b, blk):
        return [pl.BlockSpec((tm, w), lambda s: (blk(b, s), 0)),
                pl.BlockSpec((tm, EX_WIDTH), lambda s: (blk(b, s), 0))]

    n_streams = 2 * n_batch
    vmem = (2 * n_streams * tm * (w + EX_WIDTH) * 4 + 2 * n_streams * tm * DN_V_W * 4
            + n_streams * DN_HEADS * DN_DK * DN_DV * 4 + n_streams * DN_HEADS * 40 * CHUNK * DN_DK * 4)
    out = jax.ShapeDtypeStruct((n_batch, seq + tm, DN_V_W), F32)
    return pl.pallas_call(
        functools.partial(_dn_kernel, n_batch=n_batch),
        out_shape=(out, out),
        grid=(bps + 1,),
        in_specs=[spec for b in range(n_batch) for blk in (fwd, bwd) for spec in specs(b, blk)] + [
            pl.BlockSpec((2, 1, DN_HEADS), lambda s: (0, 0, 0)),
            pl.BlockSpec((2, 1, DN_HEADS), lambda s: (0, 0, 0))],
        out_specs=_seq_out_specs(n_batch, tm, DN_V_W, bps),
        scratch_shapes=[pltpu.VMEM((n_batch, 2, DN_HEADS, DN_DK, DN_DV), F32)],
        compiler_params=_cparams(1, vmem),
        name="deltanet_scan",
    )(*([x, ex] * n_streams), a_log, dt_bias)


def _attn_kernel(q_ref, kc_ref, vtc_ref, kl_ref, vtl_ref, o_ref, m_ref, acc_ref,
                 *, with_ctx, tk):
    tq = q_ref.shape[0]
    ctx_chunk = (lambda: kc_ref[...], lambda: vtc_ref[...])
    lat_chunks = [(lambda c=c: kl_ref[c * tk:(c + 1) * tk, :], lambda c=c: vtl_ref[:, c * tk:(c + 1) * tk])
                  for c in range(kl_ref.shape[0] // tk)]

    def scores(chunk, g):
        return _dot_nt(chunk[0](), q_ref[:, g * ATT_HD:(g + 1) * ATT_HD])

    def accumulate(chunk, g, s, first):
        cols = slice(g * tq, (g + 1) * tq)
        m_new = jnp.max(s, axis=0, keepdims=True)
        if not first:
            m_old = m_ref[:, cols]
            m_new = jnp.maximum(m_old, m_new)
            alpha = jnp.exp2(m_old - m_new)
        p = jnp.exp2(s - m_new).astype(BF16)
        pv = _dot(chunk[1](), p)
        if not first:
            pv = alpha * acc_ref[:, cols] + pv
        m_ref[:, cols] = m_new
        acc_ref[:, cols] = pv

    def attend(chunks):
        units = [(ci, g) for ci in range(len(chunks)) for g in range(ATT_GROUP)]
        pending = [scores(chunks[ci], g) for ci, g in units[:ATT_LOOKAHEAD]]
        for i, (ci, g) in enumerate(units):
            if i + ATT_LOOKAHEAD < len(units):
                nci, ng = units[i + ATT_LOOKAHEAD]
                pending.append(scores(chunks[nci], ng))
            accumulate(chunks[ci], g, pending.pop(0), first=ci == 0)
        for g in range(ATT_GROUP):
            cols = slice(g * tq, (g + 1) * tq)
            o = acc_ref[:ATT_HD, cols] / acc_ref[ATT_HD:ATT_HD + 1, cols]
            o_ref[:, g * ATT_HD:(g + 1) * ATT_HD] = o.T.astype(o_ref.dtype)

    def latent_query():
        attend([ctx_chunk] + lat_chunks)

    def context_query():
        attend([ctx_chunk])

    if with_ctx:
        is_ctx = pl.program_id(2) == 0
        pl.when(is_ctx)(context_query)
        pl.when(jnp.logical_not(is_ctx))(latent_query)
    else:
        latent_query()


def _attention(aq, ak, avt, *, n_batch, n_lat_rows, seq, ctx_len, with_ctx):
    n_rows = aq.shape[0]
    tq = SEQ_BLOCK
    assert ctx_len == tq
    bps = seq // tq
    nlb = n_lat_rows // tq
    tk = _pick(seq, (512, 256))
    n_q = bps + (1 if with_ctx else 0)
    q_cw = ATT_GROUP * ATT_HD

    def q_blk(b, qi):
        if with_ctx:
            return jnp.where(qi == 0, nlb + b, b * bps + qi - 1)
        return b * bps + qi

    vmem = (2 * tq * q_cw * (2 + 4) + 2 * 2 * (seq + ctx_len) * ATT_HD * 2
            + ATT_GROUP * tq * (2 * 8 + ATT_HD) * 4 + 6 * ATT_GROUP * tq * tk * 4)
    return pl.pallas_call(
        functools.partial(_attn_kernel, with_ctx=with_ctx, tk=tk),
        out_shape=jax.ShapeDtypeStruct((n_rows if with_ctx else n_lat_rows, ATT_Q_W), BF16),
        grid=(n_batch, ATT_KV_HEADS, n_q),
        in_specs=[pl.BlockSpec((tq, q_cw), lambda b, h, qi: (q_blk(b, qi), h)),
                  pl.BlockSpec((ctx_len, ATT_HD), lambda b, h, qi: (nlb + b, h)),
                  pl.BlockSpec((VT_ROWS, ctx_len), lambda b, h, qi: (h, nlb + b)),
                  pl.BlockSpec((seq, ATT_HD), lambda b, h, qi: (b, h)),
                  pl.BlockSpec((VT_ROWS, seq), lambda b, h, qi: (h, b))],
        out_specs=pl.BlockSpec((tq, q_cw), lambda b, h, qi: (q_blk(b, qi), h)),
        scratch_shapes=[pltpu.VMEM((1, ATT_GROUP * tq), F32),
                        pltpu.VMEM((VT_ROWS, ATT_GROUP * tq), F32)],
        compiler_params=_cparams(3, vmem),
        name="gqa_attention",
    )(aq, ak, avt, ak, avt)


def _outproj_kernel(x_ref, mod_ref, gf_ref, gb_ref, gr_ref, df_ref, db_ref, dg_ref, at_ref,
                    gn_ref, dn_ref, w_ref, g_ref, b_ref, o_ref, *, mod_row, alpha):
    y = _dot(at_ref[...], w_ref[GLA_V_W + DN_V_W:, :])
    row0 = 0
    for f_ref, b_ref_, gate_ref, norm_ref, heads, dv in ((gf_ref, gb_ref, gr_ref, gn_ref, GLA_HEADS, GLA_DV),
                                                         (df_ref, db_ref, dg_ref, dn_ref, DN_HEADS, DN_DV)):
        parts = []
        for h in range(heads):
            sl = slice(h * dv, (h + 1) * dv)
            gate = _silu(gate_ref[:, sl].astype(F32))
            o = _head_rmsnorm(f_ref[0, :, sl] + b_ref_[0, :, sl], norm_ref[...]) * gate
            parts.append(o.astype(BF16))
        y = y + _dot(jnp.concatenate(parts, axis=1), w_ref[row0:row0 + heads * dv, :])
        row0 += heads * dv
    gate = mod_ref[0, mod_row:mod_row + 1, :]
    o_ref[...] = _layernorm(alpha * x_ref[...] + gate * y, g_ref[...], b_ref[...])


def _outproj(x, mod, gla_f, gla_b, p, dn_f, dn_b, att, gla_norm_g, dn_norm_g, w_out, ln_g, ln_b,
             *, layer, mod_row, n_rows, n_batch, n_lat_rows, seq, alpha):
    d = x.shape[1]
    tm = SEQ_BLOCK
    idx = functools.partial(_mod_block_index, tm=tm, n_lat_rows=n_lat_rows, seq=seq, n_batch=n_batch)
    row = lambda w, cb: pl.BlockSpec((tm, w), lambda i: (i, cb))
    vec = lambda w: pl.BlockSpec((1, w), lambda i: (0, 0))
    bps = seq // tm
    nlb = n_lat_rows // tm
    scan = lambda w: pl.BlockSpec((1, tm, w), lambda i: (jnp.where(i < nlb, i // bps, i - nlb),
                                                         jnp.where(i < nlb, 1 + i % bps, 0), 0))
    vmem = (2 * tm * (2 * d + 6 * GLA_V_W + ATT_Q_W) * 4 + 2 * MIX_WIDTH * d * 2
            + tm * MIX_WIDTH * 2 + 6 * tm * d * 4)
    return pl.pallas_call(
        functools.partial(_outproj_kernel, mod_row=mod_row, alpha=alpha),
        out_shape=jax.ShapeDtypeStruct((n_rows, d), F32),
        grid=(n_rows // tm,),
        in_specs=[row(d, 0),
                  pl.BlockSpec((None, 1, N_MOD, d), lambda i: (layer, idx(i), 0, 0)),
                  scan(GLA_V_W), scan(GLA_V_W), row(GLA_V_W, OFF_GLA_R // GLA_V_W),
                  scan(DN_V_W), scan(DN_V_W), row(DN_V_W, OFF_DN_GATE // DN_V_W),
                  row(ATT_Q_W, 0),
                  vec(GLA_DV), vec(DN_DV),
                  pl.BlockSpec((None, MIX_WIDTH, d), lambda i: (layer, 0, 0)),
                  vec(d), vec(d)],
        out_specs=row(d, 0),
        compiler_params=_cparams(1, vmem),
        name="mixer_out_proj",
    )(x, mod, gla_f, gla_b, p, dn_f, dn_b, p, att, gla_norm_g, dn_norm_g, w_out, ln_g, ln_b)


def _rope_tables(seq):
    t = jnp.arange(seq)
    pr = (t // GRID_W).astype(F32)
    pc = (t % GRID_W).astype(F32)
    axis_dim = ATT_HD // 2
    inv = ROPE_THETA ** (-jnp.arange(0, axis_dim, 2, dtype=F32) / axis_dim)
    ar = pr[:, None] * inv
    ac = pc[:, None] * inv
    cos = jnp.concatenate([jnp.cos(ar), jnp.cos(ar), jnp.cos(ac), jnp.cos(ac)], axis=1)
    sin = jnp.concatenate([-jnp.sin(ar), jnp.sin(ar), -jnp.sin(ac), jnp.sin(ac)], axis=1)
    return cos, sin


def kernel(x, c, ctx, c_ctx, w_ada, b_ada, ln_g, ln_b, w_ffn_gate, w_ffn_up, w_ffn_down, w_in,
           gla_wa1, gla_wa2, gla_ba, gla_norm_g, dn_conv, dn_wab, dn_a_log, dn_dt_bias, dn_norm_g,
           q_norm_g, k_norm_g, w_out):
    n_batch, seq, d = x.shape
    ctx_len = ctx.shape[1]
    depth = w_ada.shape[0]
    alpha = (2 * depth) ** 0.25
    n_lat = n_batch * seq
    n_all = n_lat + n_batch * ctx_len
    assert n_batch + 1 <= 8 and seq % SEQ_BLOCK == 0 and ctx_len == SEQ_BLOCK

    cond = jnp.zeros((8, d), F32).at[:n_batch].set(c).at[n_batch].set(c_ctx)
    mod_all = _ada_table(cond, w_ada, b_ada).reshape(depth, 8, N_MOD, d)
    rope_cos, rope_sin = _rope_tables(seq)
    xs = x.reshape(n_lat, d)
    x_ctx = ctx.reshape(n_batch * ctx_len, d)
    ffn_w = (w_ffn_gate.astype(BF16), w_ffn_up.astype(BF16), w_ffn_down.astype(BF16))
    w_in_b = w_in.astype(BF16)
    w_out_b = w_out.astype(BF16)

    for layer in range(depth):
        last = layer == depth - 1
        mod = mod_all
        lg = ln_g[layer].reshape(3, 1, d)
        lb = ln_b[layer].reshape(3, 1, d)
        w_ex = jnp.concatenate([gla_wa1[layer, 0], gla_wa1[layer, 1], dn_wab[layer, 0], dn_wab[layer, 1]], axis=1)
        w_ex = jnp.pad(w_ex, ((0, 0), (0, EX_WIDTH - w_ex.shape[1]))).astype(BF16)
        common = dict(n_lat_rows=n_lat, seq=seq)

        xs = _ffn(xs, mod, *ffn_w, lg[0], lb[0], w_index=(layer, 0), mod_row=0, n_rows=n_all, alpha=alpha,
                  x_ctx=x_ctx if layer == 0 else None, n_batch=n_batch, **common)
        p, ex = _inproj(xs, mod, w_in_b, w_ex, layer=layer, mod_row=3, n_batch=n_batch, **common)
        dn_x, aq, ak, avt = _prep(p, dn_conv[layer], q_norm_g[layer].reshape(1, ATT_HD),
                              k_norm_g[layer].reshape(1, ATT_HD), rope_cos, rope_sin, **common)
        gla_f, gla_b = _gla(p, ex, gla_wa2[layer], gla_ba[layer].reshape(2, 1, GLA_QK_W),
                            n_batch=n_batch, **common)
        dn_f, dn_b = _dn(dn_x, ex, dn_a_log[layer].reshape(2, 1, DN_HEADS),
                         dn_dt_bias[layer].reshape(2, 1, DN_HEADS), n_batch=n_batch, **common)
        att = _attention(aq, ak, avt, n_batch=n_batch, ctx_len=ctx_len, with_ctx=not last, **common)
        n_out = n_lat if last else n_all
        xs = _outproj(xs, mod, gla_f, gla_b, p, dn_f, dn_b, att, gla_norm_g[layer].reshape(1, GLA_DV),
                      dn_norm_g[layer].reshape(1, DN_DV), w_out_b, lg[1], lb[1],
                      layer=layer, mod_row=5, n_rows=n_out, alpha=alpha, n_batch=n_batch, **common)
        xs = _ffn(xs, mod, *ffn_w, lg[2], lb[2], w_index=(layer, 1), mod_row=6, n_rows=n_out, alpha=alpha,
                  n_batch=n_batch, **common)
    return xs.reshape(n_batch, seq, d)
```

```python
import functools
import math

import jax
import jax.numpy as jnp
from jax import lax
from jax.experimental import pallas as pl
from jax.experimental.pallas import tpu as pltpu

F32 = jnp.float32
BF16 = jnp.bfloat16

N_MOD = 9
GRID_W = 64
MACARON_WEIGHT = 0.5
GLA_HEADS, GLA_DK, GLA_DV, GLA_RANK, GLA_TAU = 4, 64, 128, 16, 16.0
DN_HEADS, DN_DK, DN_DV, DN_CONV = 4, 128, 128, 3
ATT_HEADS, ATT_KV_HEADS, ATT_HD = 8, 2, 128
ATT_GROUP = ATT_HEADS // ATT_KV_HEADS
VT_ROWS = ATT_HD + 16
ATT_LOOKAHEAD = 3
CHUNK = 64
ROPE_THETA = 10000.0
NORM_EPS = 1e-6
LOG2_E = math.log2(math.e)
GLA_QK_W, GLA_V_W = GLA_HEADS * GLA_DK, GLA_HEADS * GLA_DV
DN_QK_W, DN_V_W = DN_HEADS * DN_DK, DN_HEADS * DN_DV
ATT_Q_W, ATT_KV_W = ATT_HEADS * ATT_HD, ATT_KV_HEADS * ATT_HD
IN_WIDTH = 2 * GLA_QK_W + 2 * GLA_V_W + 2 * DN_QK_W + 2 * DN_V_W + ATT_Q_W + 2 * ATT_KV_W
MIX_WIDTH = GLA_V_W + DN_V_W + ATT_Q_W
OFF_GLA_V, OFF_GLA_R = 2 * GLA_QK_W, 2 * GLA_QK_W + GLA_V_W
OFF_DN = OFF_GLA_R + GLA_V_W
OFF_DN_GATE = OFF_DN + 2 * DN_QK_W + DN_V_W
OFF_ATT = OFF_DN_GATE + DN_V_W
EX_WIDTH = 128
EX_DN = 2 * GLA_RANK

V7X_VMEM_BYTES = 64 * 1024 * 1024
LANES = 128
SEQ_BLOCK = 256
CHUNKS_PER_BLOCK = SEQ_BLOCK // CHUNK
HALO_ROWS = 16


def _cparams(n_axes, vmem_bytes):
    limit = int(min(max(vmem_bytes, 16 * 1024 * 1024), V7X_VMEM_BYTES - 8 * 1024 * 1024))
    return pltpu.CompilerParams(dimension_semantics=("arbitrary",) * n_axes,
                                vmem_limit_bytes=limit)


def _dot(a, b):
    return jnp.dot(a, b, preferred_element_type=F32)


def _dot_nt(a, b):
    return lax.dot_general(a, b, (((1,), (1,)), ((), ())), preferred_element_type=F32)


def _dot_tn(a, b):
    return lax.dot_general(a, b, (((0,), (0,)), ((), ())), preferred_element_type=F32)


def _sigmoid(x):
    return 1.0 / (1.0 + jnp.exp(-x))


def _silu(x):
    return x * _sigmoid(x)


def _softplus(x):
    return jnp.maximum(x, 0.0) + jnp.log(1.0 + jnp.exp(-jnp.abs(x)))


def _layernorm(z, g, b):
    mu = jnp.mean(z, axis=-1, keepdims=True)
    zc = z - mu
    var = jnp.mean(zc * zc, axis=-1, keepdims=True)
    return zc * lax.rsqrt(var + NORM_EPS) * g + b


def _head_rmsnorm(x, g):
    return x * lax.rsqrt(jnp.mean(x * x, axis=-1, keepdims=True) + NORM_EPS) * g


def _cumsum_f32(tri, g):
    g1 = g.astype(BF16)
    r1 = g - g1.astype(F32)
    g2 = r1.astype(BF16)
    g3 = (r1 - g2.astype(F32)).astype(BF16)
    return _dot(tri, g1) + _dot(tri, g2) + _dot(tri, g3)


def _pick(n, candidates):
    for c in candidates:
        if n % c == 0:
            return c
    raise ValueError(f"no block size in {candidates} divides {n}")


def _mod_block_index(i, tm, n_lat_rows, seq, n_batch):
    assert seq % tm == 0 and n_lat_rows % tm == 0
    r = i * tm
    return jnp.where(r < n_lat_rows, r // seq, n_batch)


def _ada_kernel(c_ref, w_ref, b_ref, o_ref):
    s = _silu(c_ref[...]).astype(BF16)
    o_ref[0] = _dot(s, w_ref[0].astype(BF16)) + b_ref[0]


def _ada_table(cond, w_ada, b_ada):
    n_layer, d, n = w_ada.shape
    tn = _pick(n, (1024, 512, 256, 128))
    vmem = 2 * d * tn * 4 + d * tn * 2 + 4 * 8 * (d + 2 * tn) * 4
    return pl.pallas_call(
        _ada_kernel,
        out_shape=jax.ShapeDtypeStruct((n_layer, 8, n), F32),
        grid=(n_layer, n // tn),
        in_specs=[pl.BlockSpec((8, d), lambda l, j: (0, 0)),
                  pl.BlockSpec((1, d, tn), lambda l, j: (l, 0, j)),
                  pl.BlockSpec((1, 1, tn), lambda l, j: (l, 0, j))],
        out_specs=pl.BlockSpec((1, 8, tn), lambda l, j: (l, 0, j)),
        compiler_params=_cparams(2, vmem),
        name="ada_table",
    )(cond, w_ada, b_ada.reshape(n_layer, 1, n))


def _ffn_kernel(*refs, mod_row, alpha, n_lat_blocks, split_x, n_round):
    if split_x:
        x_ref, xc_ref, *refs = refs
    else:
        x_ref, *refs = refs
    mod_ref, wg_ref, wu_ref, wd_ref, g_ref, b_ref, *refs = refs
    raw_refs, refs = refs[:n_round], refs[n_round:]
    o_ref, *refs = refs
    rounded_refs, (h_ref, acc_ref) = refs[:n_round], refs[n_round:]
    j = pl.program_id(1)

    for raw_ref, rounded_ref in zip(raw_refs, rounded_refs):
        rounded_ref[...] = raw_ref[...].astype(BF16)

    def load_x():
        if not split_x:
            return x_ref[...]
        return jnp.where(pl.program_id(0) < n_lat_blocks, x_ref[...], xc_ref[...])

    @pl.when(j == 0)
    def _():
        shift = mod_ref[0, mod_row:mod_row + 1, :]
        scale = mod_ref[0, mod_row + 1:mod_row + 2, :]
        h_ref[...] = (load_x() * (1.0 + scale) + shift).astype(BF16)
        acc_ref[...] = jnp.zeros(acc_ref.shape, F32)

    h = h_ref[...]
    a = _dot(h, wg_ref[...])
    u = _dot(h, wu_ref[...])
    acc_ref[...] += _dot((_silu(a) * u).astype(BF16), wd_ref[...])

    @pl.when(j == pl.num_programs(1) - 1)
    def _():
        gate = mod_ref[0, mod_row + 2:mod_row + 3, :]
        z = alpha * load_x() + (MACARON_WEIGHT * gate) * acc_ref[...]
        o_ref[...] = _layernorm(z, g_ref[...], b_ref[...])


def _ffn(x, mod, wg, wu, wd, ln_g, ln_b, *, layer, mod_row, n_rows, n_batch, n_lat_rows, seq,
         alpha, x_ctx=None, round_next=None):
    d = x.shape[1]
    f = wg.shape[-1]
    tm = _pick(n_rows, (512, 256))
    tf = _pick(f, (512, 256, 128))
    n_row_blocks, n_tiles = n_rows // tm, f // tf
    nlb = n_lat_rows // tm
    split_x = x_ctx is not None
    vmem = ((6 if split_x else 4) * tm * d * 4 + tm * d * 4 + tm * d * 2 + 2 * 3 * d * tf * 2
            + 8 * tm * tf * 4 + 2 * 16 * d * 4)
    idx = functools.partial(_mod_block_index, tm=tm, n_lat_rows=n_lat_rows, seq=seq, n_batch=n_batch)
    if split_x:
        x_specs = [pl.BlockSpec((tm, d), lambda i, j: (jnp.minimum(i, nlb - 1), 0)),
                   pl.BlockSpec((tm, d), lambda i, j: (jnp.maximum(i - nlb, 0), 0))]
        xs = (x, x_ctx)
    else:
        x_specs = [pl.BlockSpec((tm, d), lambda i, j: (i, 0))]
        xs = (x,)
    out_shape = [jax.ShapeDtypeStruct((n_rows, d), F32)]
    out_specs = [pl.BlockSpec((tm, d), lambda i, j: (i, 0))]
    raw, raw_specs = (), []
    if round_next is not None:
        *raw, nxt = round_next
        n_parts = max(nb for nb in range(1, n_row_blocks + 1) if d % (16 * nb) == 0 and tf % (16 * nb) == 0)
        step = lambda i, j: jnp.minimum(i * n_tiles + j, n_parts * n_tiles - 1)
        up_blk, dn_blk = (d // n_parts, tf), (tf // n_parts, d)
        up_idx = lambda i, j: (step(i, j) // n_tiles, step(i, j) % n_tiles)
        dn_idx = lambda i, j: (step(i, j), 0)
        raw_specs = [pl.BlockSpec((None, None) + up_blk, lambda i, j: (*nxt, *up_idx(i, j))),
                     pl.BlockSpec((None, None) + up_blk, lambda i, j: (*nxt, *up_idx(i, j))),
                     pl.BlockSpec((None, None) + dn_blk, lambda i, j: (*nxt, *dn_idx(i, j)))]
        out_shape += [jax.ShapeDtypeStruct((d, f), BF16), jax.ShapeDtypeStruct((d, f), BF16),
                      jax.ShapeDtypeStruct((f, d), BF16)]
        out_specs += [pl.BlockSpec(up_blk, up_idx), pl.BlockSpec(up_blk, up_idx), pl.BlockSpec(dn_blk, dn_idx)]
        vmem += 2 * 3 * d // n_parts * tf * (4 + 2)
    out = pl.pallas_call(
        functools.partial(_ffn_kernel, mod_row=mod_row, alpha=alpha, n_lat_blocks=nlb, split_x=split_x,
                          n_round=len(raw)),
        out_shape=out_shape,
        grid=(n_row_blocks, n_tiles),
        in_specs=x_specs + [
            pl.BlockSpec((None, 1, N_MOD, d), lambda i, j: (layer, idx(i), 0, 0)),
            pl.BlockSpec((d, tf), lambda i, j: (0, j)),
            pl.BlockSpec((d, tf), lambda i, j: (0, j)),
            pl.BlockSpec((tf, d), lambda i, j: (j, 0)),
            pl.BlockSpec((1, d), lambda i, j: (0, 0)),
            pl.BlockSpec((1, d), lambda i, j: (0, 0))] + raw_specs,
        out_specs=out_specs,
        scratch_shapes=[pltpu.VMEM((tm, d), BF16), pltpu.VMEM((tm, d), F32)],
        compiler_params=_cparams(2, vmem),
        name="ffn_sublayer",
    )(*xs, mod, wg, wu, wd, ln_g, ln_b, *raw)
    return out[0], tuple(out[1:])


def _inproj_kernel(x_ref, mod_ref, w_ref, wex_ref, p_ref, ex_ref, h_ref, *, mod_row):
    j = pl.program_id(1)

    @pl.when(j == 0)
    def _():
        shift = mod_ref[0, mod_row:mod_row + 1, :]
        scale = mod_ref[0, mod_row + 1:mod_row + 2, :]
        h = (x_ref[...] * (1.0 + scale) + shift).astype(BF16)
        h_ref[...] = h
        ex_ref[...] = _dot(h, wex_ref[...])

    p_ref[...] = _dot(h_ref[...], w_ref[...]).astype(p_ref.dtype)


def _inproj(x, mod, w_in, w_ex, *, layer, mod_row, n_batch, n_lat_rows, seq):
    n_rows, d = x.shape
    n = w_in.shape[-1]
    tm = _pick(n_rows, (1024, 512, 256))
    tn = _pick(n, (1280, 512, 256, 128))
    vmem = (2 * tm * d * 4 + tm * d * 2 + 2 * d * tn * 2 + 2 * d * EX_WIDTH * 2
            + 2 * tm * tn * 2 + 2 * tm * tn * 4 + 2 * tm * EX_WIDTH * 4 + 2 * 16 * d * 4 + 2 * tm * d * 4)
    idx = functools.partial(_mod_block_index, tm=tm, n_lat_rows=n_lat_rows, seq=seq, n_batch=n_batch)
    return pl.pallas_call(
        functools.partial(_inproj_kernel, mod_row=mod_row),
        out_shape=(jax.ShapeDtypeStruct((n_rows, n), BF16),
                   jax.ShapeDtypeStruct((n_rows, EX_WIDTH), F32)),
        grid=(n_rows // tm, n // tn),
        in_specs=[pl.BlockSpec((tm, d), lambda i, j: (i, 0)),
                  pl.BlockSpec((None, 1, N_MOD, d), lambda i, j: (layer, idx(i), 0, 0)),
                  pl.BlockSpec((None, d, tn), lambda i, j: (layer, 0, j)),
                  pl.BlockSpec((d, EX_WIDTH), lambda i, j: (0, 0))],
        out_specs=(pl.BlockSpec((tm, tn), lambda i, j: (i, j)),
                   pl.BlockSpec((tm, EX_WIDTH), lambda i, j: (i, 0))),
        scratch_shapes=[pltpu.VMEM((tm, d), BF16)],
        compiler_params=_cparams(2, vmem),
        name="mixer_in_proj",
    )(x, mod, w_in, w_ex)


def _prep_kernel(dn_ref, prev_ref, next_ref, conv_ref, aq0_ref, aq1_ref, akv_ref, qg_ref, kg_ref,
                 cos_ref, sin_ref, dn_o, aq_o, ak_o, avt_o, *, n_lat_blocks, blocks_per_seq):
    i = pl.program_id(0)
    is_lat = i < n_lat_blocks
    pos = i % blocks_per_seq
    has_prev = jnp.logical_and(is_lat, pos != 0)
    has_next = jnp.logical_and(is_lat, pos != blocks_per_seq - 1)

    z = dn_ref[...].astype(F32)
    rows = z.shape[0]
    ridx = lax.broadcasted_iota(jnp.int32, z.shape, 0)
    halo_p = jnp.where(has_prev, prev_ref[HALO_ROWS - 1:HALO_ROWS, :].astype(F32), 0.0)
    halo_n = jnp.where(has_next, next_ref[0:1, :].astype(F32), 0.0)
    z_prev = jnp.where(ridx == 0, halo_p, pltpu.roll(z, 1, 0))
    z_next = jnp.where(ridx == rows - 1, halo_n, pltpu.roll(z, rows - 1, 0))
    y = _silu(z_prev * conv_ref[0:1, :] + z * conv_ref[1:2, :] + z_next * conv_ref[2:3, :])
    for hh in range(2 * DN_HEADS):
        sl = slice(hh * DN_DK, (hh + 1) * DN_DK)
        yh = y[:, sl]
        yn = yh * lax.rsqrt(jnp.sum(yh * yh, axis=-1, keepdims=True) + NORM_EPS)
        if hh < DN_HEADS:
            yn = yn * (DN_DK ** -0.5)
        dn_o[:, sl] = yn
    dn_o[:, 2 * DN_QK_W:] = y[:, 2 * DN_QK_W:]

    lane = lax.broadcasted_iota(jnp.int32, (rows, ATT_HD), 1)
    first_half_of_pair = (lane // (ATT_HD // 4)) % 2 == 0
    cos = cos_ref[...]
    sin = sin_ref[...]

    def rope(xh):
        swapped = jnp.where(first_half_of_pair,
                            pltpu.roll(xh, ATT_HD - ATT_HD // 4, 1), pltpu.roll(xh, ATT_HD // 4, 1))
        return jnp.where(is_lat, xh * cos + swapped * sin, xh)

    for hh in range(ATT_HEADS):
        src = aq0_ref if hh < ATT_HEADS // 2 else aq1_ref
        off = (hh % (ATT_HEADS // 2)) * ATT_HD
        qh = rope(_head_rmsnorm(src[:, off:off + ATT_HD].astype(F32), qg_ref[...])) * (LOG2_E * ATT_HD ** -0.5)
        aq_o[:, hh * ATT_HD:(hh + 1) * ATT_HD] = qh.astype(BF16)
    for hh in range(ATT_KV_HEADS):
        sl = slice(hh * ATT_HD, (hh + 1) * ATT_HD)
        ak_o[:, sl] = rope(_head_rmsnorm(akv_ref[:, sl].astype(F32), kg_ref[...])).astype(BF16)
        vt = akv_ref[:, ATT_KV_W + hh * ATT_HD:ATT_KV_W + (hh + 1) * ATT_HD].astype(F32).T.astype(BF16)
        avt_o[hh * VT_ROWS:hh * VT_ROWS + ATT_HD, :] = vt
        avt_o[hh * VT_ROWS + ATT_HD:(hh + 1) * VT_ROWS, :] = jnp.ones((VT_ROWS - ATT_HD, rows), BF16)


def _prep(p, conv_w, q_norm_g, k_norm_g, rope_cos, rope_sin, *, n_lat_rows, seq):
    n_rows = p.shape[0]
    tm = SEQ_BLOCK
    n_blocks = n_rows // tm
    n_lat_blocks = n_lat_rows // tm
    bps = seq // tm
    dn_w = 2 * DN_QK_W + DN_V_W
    dn_cb = OFF_DN // dn_w
    assert OFF_DN % dn_w == 0 and OFF_ATT % 512 == 0
    att_cb = OFF_ATT // 512
    halo_per_block = tm // HALO_ROWS
    n_halo = n_rows // HALO_ROWS
    vmem = (2 * 2 * tm * dn_w * 4 + 2 * 3 * tm * 512 * 4 + 2 * tm * 1536 * 2
            + 10 * tm * dn_w * 4 + 4 * tm * LANES * 4)
    return pl.pallas_call(
        functools.partial(_prep_kernel, n_lat_blocks=n_lat_blocks, blocks_per_seq=bps),
        out_shape=(jax.ShapeDtypeStruct((n_rows, dn_w), F32),
                   jax.ShapeDtypeStruct((n_rows, ATT_Q_W), BF16),
                   jax.ShapeDtypeStruct((n_rows, ATT_KV_W), BF16),
                   jax.ShapeDtypeStruct((ATT_KV_HEADS * VT_ROWS, n_rows), BF16)),
        grid=(n_blocks,),
        in_specs=[pl.BlockSpec((tm, dn_w), lambda i: (i, dn_cb)),
                  pl.BlockSpec((HALO_ROWS, dn_w), lambda i: (jnp.maximum(i * halo_per_block - 1, 0), dn_cb)),
                  pl.BlockSpec((HALO_ROWS, dn_w),
                               lambda i: (jnp.minimum((i + 1) * halo_per_block, n_halo - 1), dn_cb)),
                  pl.BlockSpec((DN_CONV, dn_w), lambda i: (0, 0)),
                  pl.BlockSpec((tm, 512), lambda i: (i, att_cb)),
                  pl.BlockSpec((tm, 512), lambda i: (i, att_cb + 1)),
                  pl.BlockSpec((tm, 512), lambda i: (i, att_cb + 2)),
                  pl.BlockSpec((1, ATT_HD), lambda i: (0, 0)),
                  pl.BlockSpec((1, ATT_HD), lambda i: (0, 0)),
                  pl.BlockSpec((tm, ATT_HD), lambda i: (jnp.where(i < n_lat_blocks, i % bps, 0), 0)),
                  pl.BlockSpec((tm, ATT_HD), lambda i: (jnp.where(i < n_lat_blocks, i % bps, 0), 0))],
        out_specs=(pl.BlockSpec((tm, dn_w), lambda i: (i, 0)),
                   pl.BlockSpec((tm, ATT_Q_W), lambda i: (i, 0)),
                   pl.BlockSpec((tm, ATT_KV_W), lambda i: (i, 0)),
                   pl.BlockSpec((ATT_KV_HEADS * VT_ROWS, tm), lambda i: (0, i))),
        compiler_params=_cparams(1, vmem),
        name="mixer_prep",
    )(p, p, p, conv_w, p, p, p, q_norm_g, k_norm_g, rope_cos, rope_sin)


def _seq_block(b, s, *, rev, n_lat_blocks, blocks_per_seq):
    j = (blocks_per_seq - s) if rev else (s - 1)
    return jnp.where(s == 0, n_lat_blocks + b, b * blocks_per_seq + j)


def _tri_masks(rev):
    row = lax.broadcasted_iota(jnp.int32, (CHUNK, CHUNK), 0)
    col = lax.broadcasted_iota(jnp.int32, (CHUNK, CHUNK), 1)
    incl = (row <= col) if rev else (row >= col)
    strict = (row < col) if rev else (row > col)
    return incl, strict


def _gla_kernel(*refs, n_batch):
    n_in = n_batch * 2 * 4
    wa2_ref, ba_ref, of_ref, ob_ref, st_ref = refs[n_in:]
    o_refs = (of_ref, ob_ref)
    streams = [(b, d) + tuple(refs[(b * 2 + d) * 4:(b * 2 + d + 1) * 4])
               for b in range(n_batch) for d in range(2)]

    @pl.when(pl.program_id(0) == 0)
    def _():
        st_ref[...] = jnp.zeros(st_ref.shape, F32)

    masks = [_tri_masks(rev)[0] for rev in (False, True)]
    tris = [jnp.where(m, 1.0, 0.0).astype(BF16) for m in masks]
    units = [(b, d, h) for b in range(n_batch) for d in range(2) for h in range(GLA_HEADS)]
    hs = lambda h: slice(h * GLA_DK, (h + 1) * GLA_DK)
    vs = lambda h: slice(h * GLA_DV, (h + 1) * GLA_DV)

    def chunk_step(c, carry):
        prep = {}
        for b, d, q_ref, k_ref, v_ref, e_ref in streams:
            rev = d == 1
            cc = (CHUNKS_PER_BLOCK - 1 - c) if rev else c
            rows = pl.ds(pl.multiple_of(cc * CHUNK, CHUNK), CHUNK)
            low_rank = e_ref[rows, d * GLA_RANK:(d + 1) * GLA_RANK].astype(BF16)
            logit = _dot(low_rank, wa2_ref[d].astype(BF16)) + ba_ref[d]
            log_a = -_softplus(-logit) / GLA_TAU
            bcum = _cumsum_f32(tris[d], log_a)
            blast = bcum[0:1] if rev else bcum[CHUNK - 1:CHUNK]
            kc = k_ref[rows, :].astype(F32)
            prep[b, d] = dict(
                rows=rows,
                q_in=(q_ref[rows, :].astype(F32) * (GLA_DK ** -0.5) * jnp.exp(bcum)).astype(BF16),
                k_in=(kc * jnp.exp(-bcum)).astype(BF16),
                k_end=(kc * jnp.exp(blast - bcum)).astype(BF16),
                decay=jnp.exp(blast),
                v=v_ref[rows, :].astype(BF16),
                st=st_ref[b, d])
        attn = {(b, d, h): jnp.where(masks[d], _dot_nt(prep[b, d]["q_in"][:, hs(h)],
                                                        prep[b, d]["k_in"][:, hs(h)]), 0.0).astype(BF16)
                for b, d, h in units}
        inter = {(b, d, h): _dot_nt(prep[b, d]["q_in"][:, hs(h)], prep[b, d]["st"][:, hs(h)].astype(BF16))
                 for b, d, h in units}
        intra = {(b, d, h): _dot(attn[b, d, h], prep[b, d]["v"][:, vs(h)]) for b, d, h in units}
        upd = {(b, d, h): _dot_tn(prep[b, d]["v"][:, vs(h)], prep[b, d]["k_end"][:, hs(h)])
               for b, d, h in units}
        for (b, d), pr in prep.items():
            o_refs[d][b, pr["rows"], :] = jnp.concatenate(
                [intra[b, d, h] + inter[b, d, h] for h in range(GLA_HEADS)], axis=1)
            st_ref[b, d] = pr["st"] * pr["decay"] + jnp.concatenate(
                [upd[b, d, h] for h in range(GLA_HEADS)], axis=1)
        return carry

    lax.fori_loop(0, CHUNKS_PER_BLOCK, chunk_step, 0)


def _seq_out_specs(n_batch, tm, width, bps):
    return (pl.BlockSpec((n_batch, tm, width), lambda s: (0, s, 0)),
            pl.BlockSpec((n_batch, tm, width), lambda s: (0, jnp.where(s == 0, 0, bps + 1 - s), 0)))


def _gla(p, ex, wa2, ba, *, n_batch, n_lat_rows, seq):
    tm = SEQ_BLOCK
    bps = seq // tm
    nlb = n_lat_rows // tm
    fwd = functools.partial(_seq_block, rev=False, n_lat_blocks=nlb, blocks_per_seq=bps)
    bwd = functools.partial(_seq_block, rev=True, n_lat_blocks=nlb, blocks_per_seq=bps)
    v_cb = OFF_GLA_V // GLA_V_W

    def specs(b, blk):
        return [pl.BlockSpec((tm, GLA_QK_W), lambda s: (blk(b, s), 0)),
                pl.BlockSpec((tm, GLA_QK_W), lambda s: (blk(b, s), 1)),
                pl.BlockSpec((tm, GLA_V_W), lambda s: (blk(b, s), v_cb)),
                pl.BlockSpec((tm, EX_WIDTH), lambda s: (blk(b, s), 0))]

    n_streams = 2 * n_batch
    vmem = (2 * n_streams * tm * (2 * GLA_QK_W + GLA_V_W + EX_WIDTH) * 4 + 2 * n_streams * tm * GLA_V_W * 4
            + n_streams * GLA_DV * GLA_QK_W * 4 + n_streams * 40 * CHUNK * GLA_QK_W * 4)
    out = jax.ShapeDtypeStruct((n_batch, seq + tm, GLA_V_W), F32)
    return pl.pallas_call(
        functools.partial(_gla_kernel, n_batch=n_batch),
        out_shape=(out, out),
        grid=(bps + 1,),
        in_specs=[spec for b in range(n_batch) for blk in (fwd, bwd) for spec in specs(b, blk)] + [
            pl.BlockSpec((2, GLA_RANK, GLA_QK_W), lambda s: (0, 0, 0)),
            pl.BlockSpec((2, 1, GLA_QK_W), lambda s: (0, 0, 0))],
        out_specs=_seq_out_specs(n_batch, tm, GLA_V_W, bps),
        scratch_shapes=[pltpu.VMEM((n_batch, 2, GLA_DV, GLA_QK_W), F32)],
        compiler_params=_cparams(1, vmem),
        name="gla_scan",
    )(*([p, p, p, ex] * n_streams), wa2, ba)


def _dn_kernel(*refs, n_batch):
    n_in = n_batch * 2 * 2
    alog_ref, dtb_ref, of_ref, ob_ref, st_ref = refs[n_in:]
    o_refs = (of_ref, ob_ref)
    streams = [(b, d) + tuple(refs[(b * 2 + d) * 2:(b * 2 + d + 1) * 2])
               for b in range(n_batch) for d in range(2)]

    @pl.when(pl.program_id(0) == 0)
    def _():
        st_ref[...] = jnp.zeros(st_ref.shape, F32)

    masks = [_tri_masks(rev) for rev in (False, True)]
    tris = [jnp.where(m[0], 1.0, 0.0).astype(BF16) for m in masks]
    row = lax.broadcasted_iota(jnp.int32, (CHUNK, CHUNK), 0)
    col = lax.broadcasted_iota(jnp.int32, (CHUNK, CHUNK), 1)
    eye = jnp.where(row == col, 1.0, 0.0)

    def chunk_step(c, carry):
        units = []
        for b, d, x_ref, e_ref in streams:
            rev = d == 1
            cc = (CHUNKS_PER_BLOCK - 1 - c) if rev else c
            rows = pl.ds(pl.multiple_of(cc * CHUNK, CHUNK), CHUNK)
            lo = EX_DN + 2 * DN_HEADS * d
            a_in = e_ref[rows, lo:lo + DN_HEADS]
            b_in = e_ref[rows, lo + DN_HEADS:lo + 2 * DN_HEADS]
            g = -jnp.exp(alog_ref[d]) * _softplus(a_in + dtb_ref[d])
            beta_all = _sigmoid(b_in)
            gcum_all = _cumsum_f32(tris[d], g)
            for h in range(DN_HEADS):
                gcum = gcum_all[:, h:h + 1]
                glast = gcum[0:1] if rev else gcum[CHUNK - 1:CHUNK]
                gmat = jnp.broadcast_to(gcum, (CHUNK, CHUNK))
                units.append(dict(
                    b=b, d=d, h=h, rows=rows, beta=beta_all[:, h:h + 1], gcum=gcum, glast=glast,
                    decay=jnp.exp(gmat - gmat.T),
                    q=x_ref[rows, h * DN_DK:(h + 1) * DN_DK],
                    k=x_ref[rows, DN_QK_W + h * DN_DK:DN_QK_W + (h + 1) * DN_DK],
                    v=x_ref[rows, 2 * DN_QK_W + h * DN_DV:2 * DN_QK_W + (h + 1) * DN_DV]))
        for u in units:
            u["kb"] = u["k"] * u["beta"]
            u["qk"] = _dot_nt(jnp.concatenate([u["q"], u["kb"]], axis=0).astype(BF16), u["k"].astype(BF16))
        for u in units:
            incl, strict = masks[u["d"]]
            u["attn"] = jnp.where(incl, u["qk"][:CHUNK] * u["decay"], 0.0).astype(BF16)
            n = -jnp.where(strict, u["qk"][CHUNK:] * u["decay"], 0.0)
            u["t"] = eye + n
            u["n"] = n.astype(BF16)
        for u in units:
            u["p"] = _dot(u["n"], u["n"])
        for _ in range(int(math.log2(CHUNK)) - 2):
            for u in units:
                u["both"] = _dot(jnp.concatenate([u["t"], u["p"]], axis=0).astype(BF16), u["p"].astype(BF16))
            for u in units:
                u["t"] = u["t"] + u["both"][:CHUNK]
                u["p"] = u["both"][CHUNK:]
        for u in units:
            u["tp"] = _dot(u["t"].astype(BF16), u["p"].astype(BF16))
        for u in units:
            tinv = (u["t"] + u["tp"]).astype(BF16)
            egc = jnp.exp(u["gcum"])
            rhs = jnp.concatenate([u["v"] * u["beta"], u["kb"] * egc], axis=1).astype(BF16)
            u["uw"] = _dot(tinv, rhs)
            u["q_dec"] = u["q"] * egc
            u["k_end"] = (u["k"] * jnp.exp(u["glast"] - u["gcum"])).astype(BF16)
        for u in units:
            u["state"] = st_ref[u["b"], u["d"], u["h"]]
            lhs = jnp.concatenate([u["uw"][:, DN_DV:], u["q_dec"]], axis=0).astype(BF16)
            u["wq"] = _dot(lhs, u["state"].astype(BF16))
        for u in units:
            u["v_new"] = (u["uw"][:, :DN_DV] - u["wq"][:CHUNK]).astype(BF16)
            u["o"] = u["wq"][CHUNK:] + _dot(u["attn"], u["v_new"])
        for u in units:
            st_ref[u["b"], u["d"], u["h"]] = (u["state"] * jnp.exp(u["glast"])
                                              + _dot_tn(u["k_end"], u["v_new"]))
        for b, d, _, _ in streams:
            sel = [u for u in units if u["b"] == b and u["d"] == d]
            o_refs[d][b, sel[0]["rows"], :] = jnp.concatenate([u["o"] for u in sel], axis=1)
        return carry

    lax.fori_loop(0, CHUNKS_PER_BLOCK, chunk_step, 0)


def _dn(x, ex, a_log, dt_bias, *, n_batch, n_lat_rows, seq):
    w = x.shape[1]
    tm = SEQ_BLOCK
    bps = seq // tm
    nlb = n_lat_rows // tm
    fwd = functools.partial(_seq_block, rev=False, n_lat_blocks=nlb, blocks_per_seq=bps)
    bwd = functools.partial(_seq_block, rev=True, n_lat_blocks=nlb, blocks_per_seq=bps)

    def specs(b, blk):
        return [pl.BlockSpec((tm, w), lambda s: (blk(b, s), 0)),
                pl.BlockSpec((tm, EX_WIDTH), lambda s: (blk(b, s), 0))]

    n_streams = 2 * n_batch
    vmem = (2 * n_streams * tm * (w + EX_WIDTH) * 4 + 2 * n_streams * tm * DN_V_W * 4
            + n_streams * DN_HEADS * DN_DK * DN_DV * 4 + n_streams * DN_HEADS * 40 * CHUNK * DN_DK * 4)
    out = jax.ShapeDtypeStruct((n_batch, seq + tm, DN_V_W), F32)
    return pl.pallas_call(
        functools.partial(_dn_kernel, n_batch=n_batch),
        out_shape=(out, out),
        grid=(bps + 1,),
        in_specs=[spec for b in range(n_batch) for blk in (fwd, bwd) for spec in specs(b, blk)] + [
            pl.BlockSpec((2, 1, DN_HEADS), lambda s: (0, 0, 0)),
            pl.BlockSpec((2, 1, DN_HEADS), lambda s: (0, 0, 0))],
        out_specs=_seq_out_specs(n_batch, tm, DN_V_W, bps),
        scratch_shapes=[pltpu.VMEM((n_batch, 2, DN_HEADS, DN_DK, DN_DV), F32)],
        compiler_params=_cparams(1, vmem),
        name="deltanet_scan",
    )(*([x, ex] * n_streams), a_log, dt_bias)


def _attn_kernel(q_ref, kc_ref, vtc_ref, kl_ref, vtl_ref, o_ref, m_ref, acc_ref,
                 *, with_ctx, tk):
    tq = q_ref.shape[0]
    ctx_chunk = (lambda: kc_ref[...], lambda: vtc_ref[...])
    lat_chunks = [(lambda c=c: kl_ref[c * tk:(c + 1) * tk, :], lambda c=c: vtl_ref[:, c * tk:(c + 1) * tk])
                  for c in range(kl_ref.shape[0] // tk)]

    def scores(chunk, g):
        return _dot_nt(chunk[0](), q_ref[:, g * ATT_HD:(g + 1) * ATT_HD])

    def accumulate(chunk, g, s, first):
        cols = slice(g * tq, (g + 1) * tq)
        m_new = jnp.max(s, axis=0, keepdims=True)
        if not first:
            m_old = m_ref[:, cols]
            m_new = jnp.maximum(m_old, m_new)
            alpha = jnp.exp2(m_old - m_new)
        p = jnp.exp2(s - m_new).astype(BF16)
        pv = _dot(chunk[1](), p)
        if not first:
            pv = alpha * acc_ref[:, cols] + pv
        m_ref[:, cols] = m_new
        acc_ref[:, cols] = pv

    def attend(chunks):
        units = [(ci, g) for ci in range(len(chunks)) for g in range(ATT_GROUP)]
        pending = [scores(chunks[ci], g) for ci, g in units[:ATT_LOOKAHEAD]]
        for i, (ci, g) in enumerate(units):
            if i + ATT_LOOKAHEAD < len(units):
                nci, ng = units[i + ATT_LOOKAHEAD]
                pending.append(scores(chunks[nci], ng))
            accumulate(chunks[ci], g, pending.pop(0), first=ci == 0)
        for g in range(ATT_GROUP):
            cols = slice(g * tq, (g + 1) * tq)
            o = acc_ref[:ATT_HD, cols] / acc_ref[ATT_HD:ATT_HD + 1, cols]
            o_ref[:, g * ATT_HD:(g + 1) * ATT_HD] = o.T

    def latent_query():
        attend([ctx_chunk] + lat_chunks)

    def context_query():
        attend([ctx_chunk])

    if with_ctx:
        is_ctx = pl.program_id(2) == 0
        pl.when(is_ctx)(context_query)
        pl.when(jnp.logical_not(is_ctx))(latent_query)
    else:
        latent_query()


def _attention(aq, ak, avt, *, n_batch, n_lat_rows, seq, ctx_len, with_ctx):
    n_rows = aq.shape[0]
    tq = SEQ_BLOCK
    assert ctx_len == tq
    bps = seq // tq
    nlb = n_lat_rows // tq
    tk = _pick(seq, (512, 256))
    n_q = bps + (1 if with_ctx else 0)
    q_cw = ATT_GROUP * ATT_HD

    def q_blk(b, qi):
        if with_ctx:
            return jnp.where(qi == 0, nlb + b, b * bps + qi - 1)
        return b * bps + qi

    vmem = (2 * tq * q_cw * (2 + 4) + 2 * 2 * (seq + ctx_len) * ATT_HD * 2
            + ATT_GROUP * tq * (2 * 8 + ATT_HD) * 4 + 6 * ATT_GROUP * tq * tk * 4)
    return pl.pallas_call(
        functools.partial(_attn_kernel, with_ctx=with_ctx, tk=tk),
        out_shape=jax.ShapeDtypeStruct((n_rows if with_ctx else n_lat_rows, ATT_Q_W), F32),
        grid=(n_batch, ATT_KV_HEADS, n_q),
        in_specs=[pl.BlockSpec((tq, q_cw), lambda b, h, qi: (q_blk(b, qi), h)),
                  pl.BlockSpec((ctx_len, ATT_HD), lambda b, h, qi: (nlb + b, h)),
                  pl.BlockSpec((VT_ROWS, ctx_len), lambda b, h, qi: (h, nlb + b)),
                  pl.BlockSpec((seq, ATT_HD), lambda b, h, qi: (b, h)),
                  pl.BlockSpec((VT_ROWS, seq), lambda b, h, qi: (h, b))],
        out_specs=pl.BlockSpec((tq, q_cw), lambda b, h, qi: (q_blk(b, qi), h)),
        scratch_shapes=[pltpu.VMEM((1, ATT_GROUP * tq), F32),
                        pltpu.VMEM((VT_ROWS, ATT_GROUP * tq), F32)],
        compiler_params=_cparams(3, vmem),
        name="gqa_attention",
    )(aq, ak, avt, ak, avt)


def _outproj_kernel(x_ref, mod_ref, gf_ref, gb_ref, gr_ref, df_ref, db_ref, dg_ref, at_ref,
                    gn_ref, dn_ref, w_ref, g_ref, b_ref, o_ref, *, mod_row, alpha):
    y = _dot(at_ref[...].astype(BF16), w_ref[GLA_V_W + DN_V_W:, :])
    row0 = 0
    for f_ref, b_ref_, gate_ref, norm_ref, heads, dv in ((gf_ref, gb_ref, gr_ref, gn_ref, GLA_HEADS, GLA_DV),
                                                         (df_ref, db_ref, dg_ref, dn_ref, DN_HEADS, DN_DV)):
        parts = []
        for h in range(heads):
            sl = slice(h * dv, (h + 1) * dv)
            gate = _silu(gate_ref[:, sl].astype(F32))
            o = _head_rmsnorm(f_ref[0, :, sl] + b_ref_[0, :, sl], norm_ref[...]) * gate
            parts.append(o.astype(BF16))
        y = y + _dot(jnp.concatenate(parts, axis=1), w_ref[row0:row0 + heads * dv, :])
        row0 += heads * dv
    gate = mod_ref[0, mod_row:mod_row + 1, :]
    o_ref[...] = _layernorm(alpha * x_ref[...] + gate * y, g_ref[...], b_ref[...])


def _outproj(x, mod, gla_f, gla_b, p, dn_f, dn_b, att, gla_norm_g, dn_norm_g, w_out, ln_g, ln_b,
             *, layer, mod_row, n_rows, n_batch, n_lat_rows, seq, alpha):
    d = x.shape[1]
    tm = SEQ_BLOCK
    idx = functools.partial(_mod_block_index, tm=tm, n_lat_rows=n_lat_rows, seq=seq, n_batch=n_batch)
    row = lambda w, cb: pl.BlockSpec((tm, w), lambda i: (i, cb))
    vec = lambda w: pl.BlockSpec((1, w), lambda i: (0, 0))
    bps = seq // tm
    nlb = n_lat_rows // tm
    scan = lambda w: pl.BlockSpec((1, tm, w), lambda i: (jnp.where(i < nlb, i // bps, i - nlb),
                                                         jnp.where(i < nlb, 1 + i % bps, 0), 0))
    vmem = (2 * tm * (2 * d + 6 * GLA_V_W + ATT_Q_W) * 4 + 2 * MIX_WIDTH * d * 2
            + tm * MIX_WIDTH * 2 + 6 * tm * d * 4)
    return pl.pallas_call(
        functools.partial(_outproj_kernel, mod_row=mod_row, alpha=alpha),
        out_shape=jax.ShapeDtypeStruct((n_rows, d), F32),
        grid=(n_rows // tm,),
        in_specs=[row(d, 0),
                  pl.BlockSpec((None, 1, N_MOD, d), lambda i: (layer, idx(i), 0, 0)),
                  scan(GLA_V_W), scan(GLA_V_W), row(GLA_V_W, OFF_GLA_R // GLA_V_W),
                  scan(DN_V_W), scan(DN_V_W), row(DN_V_W, OFF_DN_GATE // DN_V_W),
                  row(ATT_Q_W, 0),
                  vec(GLA_DV), vec(DN_DV),
                  pl.BlockSpec((None, MIX_WIDTH, d), lambda i: (layer, 0, 0)),
                  vec(d), vec(d)],
        out_specs=row(d, 0),
        compiler_params=_cparams(1, vmem),
        name="mixer_out_proj",
    )(x, mod, gla_f, gla_b, p, dn_f, dn_b, p, att, gla_norm_g, dn_norm_g, w_out, ln_g, ln_b)


def _rope_tables(seq):
    t = jnp.arange(seq)
    pr = (t // GRID_W).astype(F32)
    pc = (t % GRID_W).astype(F32)
    axis_dim = ATT_HD // 2
    inv = ROPE_THETA ** (-jnp.arange(0, axis_dim, 2, dtype=F32) / axis_dim)
    ar = pr[:, None] * inv
    ac = pc[:, None] * inv
    cos = jnp.concatenate([jnp.cos(ar), jnp.cos(ar), jnp.cos(ac), jnp.cos(ac)], axis=1)
    sin = jnp.concatenate([-jnp.sin(ar), jnp.sin(ar), -jnp.sin(ac), jnp.sin(ac)], axis=1)
    return cos, sin


def kernel(x, c, ctx, c_ctx, w_ada, b_ada, ln_g, ln_b, w_ffn_gate, w_ffn_up, w_ffn_down, w_in,
           gla_wa1, gla_wa2, gla_ba, gla_norm_g, dn_conv, dn_wab, dn_a_log, dn_dt_bias, dn_norm_g,
           q_norm_g, k_norm_g, w_out):
    n_batch, seq, d = x.shape
    ctx_len = ctx.shape[1]
    depth = w_ada.shape[0]
    alpha = (2 * depth) ** 0.25
    n_lat = n_batch * seq
    n_all = n_lat + n_batch * ctx_len
    assert n_batch + 1 <= 8 and seq % SEQ_BLOCK == 0 and ctx_len == SEQ_BLOCK

    cond = jnp.zeros((8, d), F32).at[:n_batch].set(c).at[n_batch].set(c_ctx)
    mod_all = _ada_table(cond, w_ada, b_ada).reshape(depth, 8, N_MOD, d)
    rope_cos, rope_sin = _rope_tables(seq)
    xs = x.reshape(n_lat, d)
    x_ctx = ctx.reshape(n_batch * ctx_len, d)
    ffn_raw = (w_ffn_gate, w_ffn_up, w_ffn_down)
    ffn_w = tuple(w[0, 0].astype(BF16) for w in ffn_raw)
    w_in_b = w_in.astype(BF16)
    w_out_b = w_out.astype(BF16)

    for layer in range(depth):
        last = layer == depth - 1
        mod = mod_all
        lg = ln_g[layer].reshape(3, 1, d)
        lb = ln_b[layer].reshape(3, 1, d)
        w_ex = jnp.concatenate([gla_wa1[layer, 0], gla_wa1[layer, 1], dn_wab[layer, 0], dn_wab[layer, 1]], axis=1)
        w_ex = jnp.pad(w_ex, ((0, 0), (0, EX_WIDTH - w_ex.shape[1]))).astype(BF16)
        common = dict(n_lat_rows=n_lat, seq=seq)

        xs, ffn_w = _ffn(xs, mod, *ffn_w, lg[0], lb[0], layer=layer, mod_row=0, n_rows=n_all, alpha=alpha,
                         x_ctx=x_ctx if layer == 0 else None, round_next=(*ffn_raw, (layer, 1)),
                         n_batch=n_batch, **common)
        p, ex = _inproj(xs, mod, w_in_b, w_ex, layer=layer, mod_row=3, n_batch=n_batch, **common)
        dn_x, aq, ak, avt = _prep(p, dn_conv[layer], q_norm_g[layer].reshape(1, ATT_HD),
                              k_norm_g[layer].reshape(1, ATT_HD), rope_cos, rope_sin, **common)
        gla_f, gla_b = _gla(p, ex, gla_wa2[layer], gla_ba[layer].reshape(2, 1, GLA_QK_W),
                            n_batch=n_batch, **common)
        dn_f, dn_b = _dn(dn_x, ex, dn_a_log[layer].reshape(2, 1, DN_HEADS),
                         dn_dt_bias[layer].reshape(2, 1, DN_HEADS), n_batch=n_batch, **common)
        att = _attention(aq, ak, avt, n_batch=n_batch, ctx_len=ctx_len, with_ctx=not last, **common)
        n_out = n_lat if last else n_all
        xs = _outproj(xs, mod, gla_f, gla_b, p, dn_f, dn_b, att, gla_norm_g[layer].reshape(1, GLA_DV),
                      dn_norm_g[layer].reshape(1, DN_DV), w_out_b, lg[1], lb[1],
                      layer=layer, mod_row=5, n_rows=n_out, alpha=alpha, n_batch=n_batch, **common)
        xs, ffn_w = _ffn(xs, mod, *ffn_w, lg[2], lb[2], layer=layer, mod_row=6, n_rows=n_out, alpha=alpha,
                         round_next=None if last else (*ffn_raw, (layer + 1, 0)), n_batch=n_batch, **common)
    return xs.reshape(n_batch, seq, d)
```
